```python
import math
import jax
import jax.numpy as jnp
from jax import lax
import numpy as np

D_MODEL = 1024
BATCH = 8
SEQ = 2048
DEPTH = 1
DEC_BATCH = 32
DEC_SEQ = 64
PAST_LEN = 4096

CHUNK = 64
D_LRU = 1024
LRU_HEADS = 16
LRU_HEAD_DIM = D_LRU // LRU_HEADS
LRU_CONV = 4
RG_C = 8.0
D_CONV = 1024
CCM_KERNEL = 31
D_FF = ((8 * D_MODEL // 3 + 255) // 256) * 256
D_IN = 2 * D_LRU + 2 * D_CONV + 2 * D_MODEL
EPS = 1e-6

kernel_name = "hawk_conformer_parallel_stream_step"


def rms_norm(x, g):
    xf = x.astype(jnp.float32)
    y = xf * lax.rsqrt(jnp.mean(xf * xf, axis=-1, keepdims=True) + EPS)
    return (y * g.astype(jnp.float32)).astype(x.dtype)


def layer_norm(x, g, b):
    xf = x.astype(jnp.float32)
    mu = jnp.mean(xf, axis=-1, keepdims=True)
    var = jnp.mean(jnp.square(xf - mu), axis=-1, keepdims=True)
    y = (xf - mu) * lax.rsqrt(var + EPS)
    return (y * g.astype(jnp.float32) + b.astype(jnp.float32)).astype(x.dtype)


def causal_dwconv(x_pad, w, b):
    c = x_pad.shape[-1]
    out = lax.conv_general_dilated(
        x_pad, w[:, None, :].astype(x_pad.dtype), window_strides=(1,), padding="VALID",
        dimension_numbers=("NWC", "WIO", "NWC"), feature_group_count=c)
    return out + b


def rg_lru(xc, h0, w_r, b_r, w_i, b_i, lam):
    bsz, t, _ = xc.shape
    xh = xc.reshape(bsz, t, LRU_HEADS, LRU_HEAD_DIM)
    r = jax.nn.sigmoid((jnp.einsum("bthd,hde->bthe", xh, w_r).reshape(bsz, t, D_LRU) + b_r).astype(jnp.float32))
    i = jax.nn.sigmoid((jnp.einsum("bthd,hde->bthe", xh, w_i).reshape(bsz, t, D_LRU) + b_i).astype(jnp.float32))
    log_a = -RG_C * r * jax.nn.softplus(-lam.astype(jnp.float32))
    a = jnp.exp(log_a)
    u = jnp.sqrt(-jnp.expm1(2.0 * log_a)) * (i * xc.astype(jnp.float32))
    u = u.at[:, 0].add(a[:, 0] * h0.astype(jnp.float32))

    def combine(c1, c2):
        a1, b1 = c1
        a2, b2 = c2
        return a1 * a2, a2 * b1 + b2

    _, hs = lax.associative_scan(combine, (a, u), axis=1)
    return hs, hs[:, -1]


def encoder_layer(x, h0, lru_buf, ccm_buf, g_mix, w_in, b_in, w_lru_conv, b_lru_conv,
                  w_rg_r, b_rg_r, w_rg_i, b_rg_i, lru_lambda, w_lru_o, w_ccm_dw, b_ccm_dw,
                  g_ccm_ln, b_ccm_ln, w_ccm_o, w_out, g_ffn, w_ffn_gate, w_ffn_up, w_ffn_down):
    h = rms_norm(x, g_mix)
    proj = jnp.einsum("btd,de->bte", h, w_in) + b_in
    splits = [D_LRU, 2 * D_LRU, 2 * D_LRU + D_CONV, 2 * D_LRU + 2 * D_CONV,
              2 * D_LRU + 2 * D_CONV + D_MODEL]
    xl, gl, ca, cb, s_lru, s_ccm = jnp.split(proj, splits, axis=-1)

    xl_pad = jnp.concatenate([lru_buf.astype(xl.dtype), xl], axis=1)
    new_lru_buf = xl_pad[:, -(LRU_CONV - 1):]
    xc = causal_dwconv(xl_pad, w_lru_conv, b_lru_conv)
    hs, h_last = rg_lru(xc, h0, w_rg_r, b_rg_r, w_rg_i, b_rg_i, lru_lambda)
    lru_out = jnp.einsum("bte,ed->btd", jax.nn.gelu(gl) * hs.astype(x.dtype), w_lru_o)

    u = ca * jax.nn.sigmoid(cb)
    u_pad = jnp.concatenate([ccm_buf.astype(u.dtype), u], axis=1)
    new_ccm_buf = u_pad[:, -(CCM_KERNEL - 1):]
    d = causal_dwconv(u_pad, w_ccm_dw, b_ccm_dw)
    d = jax.nn.silu(layer_norm(d, g_ccm_ln, b_ccm_ln))
    ccm_out = jnp.einsum("btc,cd->btd", d, w_ccm_o)

    merged = jax.nn.sigmoid(s_lru) * lru_out + jax.nn.sigmoid(s_ccm) * ccm_out
    x = x + jnp.einsum("btd,de->bte", merged, w_out)

    h2 = rms_norm(x, g_ffn)
    ff = jax.nn.silu(jnp.einsum("btd,df->btf", h2, w_ffn_gate)) * jnp.einsum("btd,df->btf", h2, w_ffn_up)
    x = x + jnp.einsum("btf,fd->btd", ff, w_ffn_down)
    return x, h_last.astype(x.dtype), new_lru_buf, new_ccm_buf


def run_trunk(x, h0s, lru_bufs, ccm_bufs, layer_params, g_final):
    hs, lbs, cbs = [], [], []
    for l in range(DEPTH):
        p = [w[l] for w in layer_params]
        x, h_last, lb, cb = encoder_layer(x, h0s[l], lru_bufs[l], ccm_bufs[l], *p)
        hs.append(h_last)
        lbs.append(lb)
        cbs.append(cb)
    return rms_norm(x, g_final), jnp.stack(hs), jnp.stack(lbs), jnp.stack(cbs)


def setup_inputs(seed: int = 0) -> dict:
    key = jax.random.key(seed)
    ks = jax.random.split(key, 32)
    f32 = jnp.float32
    nrm = lambda k, shape, s: jax.random.normal(k, shape, f32) * s
    u = jax.random.uniform(ks[12], (DEPTH, D_LRU), f32, 0.9, 0.999)
    sa = u ** (1.0 / RG_C)
    lam = jnp.log(sa) - jnp.log1p(-sa)
    return {
        "x_prompt": nrm(ks[0], (BATCH, SEQ, D_MODEL), 1.0),
        "x_sample": nrm(ks[1], (DEC_BATCH, DEC_SEQ, D_MODEL), 1.0),
        "state_lru_h": nrm(ks[2], (DEPTH, DEC_BATCH, D_LRU), 0.5),
        "cache_lru_conv": nrm(ks[3], (DEPTH, DEC_BATCH, LRU_CONV - 1, D_LRU), 1.0),
        "cache_ccm_conv": nrm(ks[4], (DEPTH, DEC_BATCH, CCM_KERNEL - 1, D_CONV), 0.5),
        "g_mix": 1.0 + nrm(ks[5], (DEPTH, D_MODEL), 0.02),
        "w_in": nrm(ks[6], (DEPTH, D_MODEL, D_IN), D_MODEL ** -0.5),
        "b_in": nrm(ks[7], (DEPTH, D_IN), 0.02),
        "w_lru_conv": nrm(ks[8], (DEPTH, LRU_CONV, D_LRU), LRU_CONV ** -0.5),
        "b_lru_conv": nrm(ks[9], (DEPTH, D_LRU), 0.02),
        "w_rg_r": nrm(ks[10], (DEPTH, LRU_HEADS, LRU_HEAD_DIM, LRU_HEAD_DIM), LRU_HEAD_DIM ** -0.5),
        "b_rg_r": nrm(ks[11], (DEPTH, D_LRU), 0.02),
        "w_rg_i": nrm(ks[13], (DEPTH, LRU_HEADS, LRU_HEAD_DIM, LRU_HEAD_DIM), LRU_HEAD_DIM ** -0.5),
        "b_rg_i": nrm(ks[14], (DEPTH, D_LRU), 0.02),
        "lru_lambda": lam,
        "w_lru_o": nrm(ks[15], (DEPTH, D_LRU, D_MODEL), D_LRU ** -0.5),
        "w_ccm_dw": nrm(ks[16], (DEPTH, CCM_KERNEL, D_CONV), CCM_KERNEL ** -0.5),
        "b_ccm_dw": nrm(ks[17], (DEPTH, D_CONV), 0.02),
        "g_ccm_ln": 1.0 + nrm(ks[18], (DEPTH, D_CONV), 0.02),
        "b_ccm_ln": nrm(ks[19], (DEPTH, D_CONV), 0.02),
        "w_ccm_o": nrm(ks[20], (DEPTH, D_CONV, D_MODEL), D_CONV ** -0.5),
        "w_out": nrm(ks[21], (DEPTH, D_MODEL, D_MODEL), D_MODEL ** -0.5),
        "g_ffn": 1.0 + nrm(ks[22], (DEPTH, D_MODEL), 0.02),
        "w_ffn_gate": nrm(ks[23], (DEPTH, D_MODEL, D_FF), D_MODEL ** -0.5),
        "w_ffn_up": nrm(ks[24], (DEPTH, D_MODEL, D_FF), D_MODEL ** -0.5),
        "w_ffn_down": nrm(ks[25], (DEPTH, D_FF, D_MODEL), D_FF ** -0.5),
        "g_final": 1.0 + nrm(ks[26], (D_MODEL,), 0.02),
    }


def reference(x_prompt, x_sample, state_lru_h, cache_lru_conv, cache_ccm_conv,
              g_mix, w_in, b_in, w_lru_conv, b_lru_conv, w_rg_r, b_rg_r, w_rg_i, b_rg_i,
              lru_lambda, w_lru_o, w_ccm_dw, b_ccm_dw, g_ccm_ln, b_ccm_ln, w_ccm_o, w_out,
              g_ffn, w_ffn_gate, w_ffn_up, w_ffn_down, g_final):
    layer_params = (g_mix, w_in, b_in, w_lru_conv, b_lru_conv, w_rg_r, b_rg_r, w_rg_i, b_rg_i,
                    lru_lambda, w_lru_o, w_ccm_dw, b_ccm_dw, g_ccm_ln, b_ccm_ln, w_ccm_o, w_out,
                    g_ffn, w_ffn_gate, w_ffn_up, w_ffn_down)
    bp = x_prompt.shape[0]
    dt = x_prompt.dtype
    h0_p = jnp.zeros((DEPTH, bp, D_LRU), dt)
    lb_p = jnp.zeros((DEPTH, bp, LRU_CONV - 1, D_LRU), dt)
    cb_p = jnp.zeros((DEPTH, bp, CCM_KERNEL - 1, D_CONV), dt)
    y_prompt, p_h, p_lb, p_cb = run_trunk(x_prompt, h0_p, lb_p, cb_p, layer_params, g_final)
    y_sample, s_h, s_lb, s_cb = run_trunk(x_sample, state_lru_h, cache_lru_conv, cache_ccm_conv,
                                          layer_params, g_final)
    return (y_prompt, y_sample, p_h, p_lb, p_cb, s_h, s_lb, s_cb)
```

```python
import functools

import jax
import jax.numpy as jnp
from jax import lax
from jax.experimental import pallas as pl
from jax.experimental.pallas import tpu as pltpu

D_MODEL = 1024
D_LRU = 1024
D_CONV = 1024
LRU_HEADS = 16
LRU_HEAD_DIM = D_LRU // LRU_HEADS
LRU_CONV = 4
CCM_KERNEL = 31
RG_C = 8.0
EPS = 1e-6
N_SPLITS = 6

V7X_MXU_DIM = 256
V7X_SUBLANES = 8
V7X_VMEM_BYTES = 64 * 1024 * 1024
GATE_BLOCK = V7X_MXU_DIM
ROW_TILE = 256

_F32 = jnp.float32
_BF16 = jnp.bfloat16


def _dot(a, b):
    return jnp.dot(a, b, preferred_element_type=_F32)


def _rms_norm(x, g):
    return x * lax.rsqrt(jnp.mean(x * x, axis=-1, keepdims=True) + EPS) * g


def _mixer_kernel(x_ref, h0_ref, lb0_ref, cb0_ref,
                  g_mix_ref, w_in_ref, b_in_ref, w_lc_ref, b_lc_ref,
                  w_r_ref, b_r_ref, w_i_ref, b_i_ref, lam_ref, w_lo_ref,
                  w_dw_ref, b_dw_ref, g_ln_ref, b_ln_ref, w_co_ref, w_out_ref,
                  x1_ref, h_out_ref, lb_out_ref, cb_out_ref,
                  lru_buf, ccm_buf, a_buf, hs_buf, h_state, *, nb, tt):
    rows = nb * tt
    lru_hist = (LRU_CONV - 1) * nb
    ccm_hist = (CCM_KERNEL - 1) * nb

    @pl.when(pl.program_id(0) == 0)
    def _load_carried_state():
        lru_buf[0:lru_hist, :] = lb0_ref[...]
        ccm_buf[0:ccm_hist, :] = cb0_ref[...]
        h_state[...] = h0_ref[...]

    x = x_ref[...]
    h = _rms_norm(x, g_mix_ref[...]).astype(_BF16)

    def proj(j):
        cols = slice(j * D_MODEL, (j + 1) * D_MODEL)
        return _dot(h, w_in_ref[:, cols]) + b_in_ref[:, cols]

    lru_buf[lru_hist:lru_hist + rows, :] = proj(0)
    xc = b_lc_ref[...] + w_lc_ref[0:1, :] * lru_buf[0:rows, :]
    for k in range(1, LRU_CONV):
        xc = xc + w_lc_ref[k:k + 1, :] * lru_buf[k * nb:k * nb + rows, :]
    xc_b = xc.astype(_BF16)

    def block_diag_dot(w_ref):
        parts = []
        for g in range(D_LRU // GATE_BLOCK):
            cols = slice(g * GATE_BLOCK, (g + 1) * GATE_BLOCK)
            parts.append(_dot(xc_b[:, cols], w_ref[g]))
        return jnp.concatenate(parts, axis=-1)

    r = jax.nn.sigmoid(block_diag_dot(w_r_ref) + b_r_ref[...])
    i = jax.nn.sigmoid(block_diag_dot(w_i_ref) + b_i_ref[...])
    neg_lam = -lam_ref[...]
    softplus = jnp.maximum(neg_lam, 0.0) + jnp.log1p(jnp.exp(-jnp.abs(neg_lam)))
    log_a = (-RG_C * softplus) * r
    a = jnp.exp(log_a)
    a_buf[...] = a
    hs_buf[...] = jnp.sqrt(1.0 - a * a) * (i * xc)

    scan_cols = D_LRU * V7X_SUBLANES // nb
    for c in range(D_LRU // scan_cols):
        cols = slice(c * scan_cols, (c + 1) * scan_cols)

        def scan_step(t, carry, cols=cols):
            r0 = pl.multiple_of(t * nb, nb)
            new = a_buf[pl.ds(r0, nb), cols] * carry + hs_buf[pl.ds(r0, nb), cols]
            hs_buf[pl.ds(r0, nb), cols] = new
            return new

        h_state[:, cols] = lax.fori_loop(0, tt, scan_step, h_state[:, cols])

    gated = (jax.nn.gelu(proj(1)) * hs_buf[...]).astype(_BF16)
    lru_out = _dot(gated, w_lo_ref[...])

    ccm_buf[ccm_hist:ccm_hist + rows, :] = proj(2) * jax.nn.sigmoid(proj(3))
    d = b_dw_ref[...] + w_dw_ref[0:1, :] * ccm_buf[0:rows, :]
    for k in range(1, CCM_KERNEL):
        d = d + w_dw_ref[k:k + 1, :] * ccm_buf[k * nb:k * nb + rows, :]
    mu = jnp.mean(d, axis=-1, keepdims=True)
    dc = d - mu
    var = jnp.mean(dc * dc, axis=-1, keepdims=True)
    dn = dc * lax.rsqrt(var + EPS) * g_ln_ref[...] + b_ln_ref[...]
    dn = dn * jax.nn.sigmoid(dn)
    ccm_out = _dot(dn.astype(_BF16), w_co_ref[...])

    merged = jax.nn.sigmoid(proj(4)) * lru_out + jax.nn.sigmoid(proj(5)) * ccm_out
    x1_ref[...] = x + _dot(merged.astype(_BF16), w_out_ref[...])

    lru_tail = lru_buf[rows:rows + lru_hist, :]
    ccm_tail = ccm_buf[rows:rows + ccm_hist, :]
    lru_buf[0:lru_hist, :] = lru_tail
    ccm_buf[0:ccm_hist, :] = ccm_tail
    lb_out_ref[...] = lru_tail
    cb_out_ref[...] = ccm_tail
    h_out_ref[...] = h_state[...]


def _ffn_kernel(x1_ref, g_ffn_ref, w_gate_ref, w_up_ref, w_down_ref, g_final_ref, y_ref):
    x1 = x1_ref[...]
    h2 = _rms_norm(x1, g_ffn_ref[...]).astype(_BF16)
    gate = _dot(h2, w_gate_ref[...])
    ff = (gate * jax.nn.sigmoid(gate) * _dot(h2, w_up_ref[...])).astype(_BF16)
    x2 = x1 + _dot(ff, w_down_ref[...])
    y_ref[...] = _rms_norm(x2, g_final_ref[...])


def _resident(shape):
    return pl.BlockSpec(shape, lambda s: (0,) * len(shape), pipeline_mode=pl.Buffered(1))


def _nbytes(shape, dtype):
    n = jnp.dtype(dtype).itemsize
    for s in shape:
        n *= s
    return n


def _vmem_limit(resident_bytes, tile_bytes, scratch_bytes, live_tiles):
    need = resident_bytes + 2 * tile_bytes + scratch_bytes + live_tiles
    return min(int(need), V7X_VMEM_BYTES)


def _mixer_call(x_tm, h0, lb0, cb0, params, *, nb, tt):
    n_rows = x_tm.shape[0]
    rows = nb * tt
    lru_hist = (LRU_CONV - 1) * nb
    ccm_hist = (CCM_KERNEL - 1) * nb
    row_tile = pl.BlockSpec((rows, D_MODEL), lambda s: (s, 0))
    resident = [h0, lb0, cb0] + list(params)
    scratch = [
        ((lru_hist + rows, D_LRU), _F32),
        ((ccm_hist + rows, D_CONV), _F32),
        ((rows, D_LRU), _F32),
        ((rows, D_LRU), _F32),
        ((nb, D_LRU), _F32),
    ]
    out_shapes = [(n_rows, D_MODEL), (nb, D_LRU), (lru_hist, D_LRU), (ccm_hist, D_CONV)]
    tile_bytes = 2 * _nbytes((rows, D_MODEL), _F32)
    limit = _vmem_limit(
        sum(_nbytes(a.shape, a.dtype) for a in resident)
        + sum(_nbytes(s, _F32) for s in out_shapes[1:]),
        tile_bytes,
        sum(_nbytes(s, d) for s, d in scratch),
        16 * _nbytes((rows, D_MODEL), _F32))
    return pl.pallas_call(
        functools.partial(_mixer_kernel, nb=nb, tt=tt),
        grid=(n_rows // rows,),
        in_specs=[row_tile] + [_resident(a.shape) for a in resident],
        out_specs=[row_tile] + [pl.BlockSpec(s, lambda i: (0, 0)) for s in out_shapes[1:]],
        out_shape=[jax.ShapeDtypeStruct(s, _F32) for s in out_shapes],
        scratch_shapes=[pltpu.VMEM(s, d) for s, d in scratch],
        compiler_params=pltpu.CompilerParams(
            dimension_semantics=("arbitrary",), vmem_limit_bytes=limit),
        name="mixer",
    )(x_tm, *resident)


def _ffn_call(x1, params):
    n_rows = x1.shape[0]
    d_ff = params[1].shape[1]
    row_tile = pl.BlockSpec((ROW_TILE, D_MODEL), lambda s: (s, 0))
    limit = _vmem_limit(
        sum(_nbytes(a.shape, a.dtype) for a in params),
        2 * _nbytes((ROW_TILE, D_MODEL), _F32),
        0,
        6 * _nbytes((ROW_TILE, d_ff), _F32))
    return pl.pallas_call(
        _ffn_kernel,
        grid=(n_rows // ROW_TILE,),
        in_specs=[row_tile] + [_resident(a.shape) for a in params],
        out_specs=row_tile,
        out_shape=jax.ShapeDtypeStruct((n_rows, D_MODEL), _F32),
        compiler_params=pltpu.CompilerParams(
            dimension_semantics=("arbitrary",), vmem_limit_bytes=limit),
        name="ffn",
    )(x1, *params)


def _pack_gate_heads(w):
    per = GATE_BLOCK // LRU_HEAD_DIM
    w = w.reshape(LRU_HEADS // per, per, LRU_HEAD_DIM, LRU_HEAD_DIM)
    eye = jnp.eye(per, dtype=w.dtype)
    blocks = jnp.einsum("gpde,pq->gpdqe", w, eye)
    return blocks.reshape(LRU_HEADS // per, GATE_BLOCK, GATE_BLOCK).astype(_BF16)


def _time_major(a):
    b, t, c = a.shape
    return jnp.swapaxes(a, 0, 1).reshape(t * b, c)


def _stream_major(a, nb):
    return jnp.swapaxes(a.reshape(-1, nb, a.shape[-1]), 0, 1)


def _trunk(x, h0, lru_cache, ccm_cache, mixer_params, ffn_params):
    nb, seq, _ = x.shape
    assert ROW_TILE % nb == 0 and nb % V7X_SUBLANES == 0
    tt = ROW_TILE // nb
    assert seq % tt == 0
    x1, h_last, lru_tail, ccm_tail = _mixer_call(
        _time_major(x), h0, _time_major(lru_cache), _time_major(ccm_cache),
        mixer_params, nb=nb, tt=tt)
    y = _ffn_call(x1, ffn_params)
    return (_stream_major(y, nb), h_last[None], _stream_major(lru_tail, nb)[None],
            _stream_major(ccm_tail, nb)[None])


def kernel(x_prompt, x_sample, state_lru_h, cache_lru_conv, cache_ccm_conv, g_mix, w_in, b_in, w_lru_conv, b_lru_conv, w_rg_r, b_rg_r, w_rg_i, b_rg_i, lru_lambda, w_lru_o, w_ccm_dw, b_ccm_dw, g_ccm_ln, b_ccm_ln, w_ccm_o, w_out, g_ffn, w_ffn_gate, w_ffn_up, w_ffn_down, g_final):
    assert g_mix.shape[0] == 1, "single-layer trunk"
    row = lambda v: v.reshape(1, -1)
    mixer_params = (
        row(g_mix[0]), w_in[0].astype(_BF16), row(b_in[0]), w_lru_conv[0], row(b_lru_conv[0]),
        _pack_gate_heads(w_rg_r[0]), row(b_rg_r[0]), _pack_gate_heads(w_rg_i[0]), row(b_rg_i[0]),
        row(lru_lambda[0]), w_lru_o[0].astype(_BF16),
        w_ccm_dw[0], row(b_ccm_dw[0]), row(g_ccm_ln[0]), row(b_ccm_ln[0]),
        w_ccm_o[0].astype(_BF16), w_out[0].astype(_BF16))
    ffn_params = (
        row(g_ffn[0]), w_ffn_gate[0].astype(_BF16), w_ffn_up[0].astype(_BF16),
        w_ffn_down[0].astype(_BF16), row(g_final))

    bp = x_prompt.shape[0]
    dt = x_prompt.dtype
    y_p, p_h, p_lb, p_cb = _trunk(
        x_prompt, jnp.zeros((bp, D_LRU), dt),
        jnp.zeros((bp, LRU_CONV - 1, D_LRU), dt), jnp.zeros((bp, CCM_KERNEL - 1, D_CONV), dt),
        mixer_params, ffn_params)
    y_s, s_h, s_lb, s_cb = _trunk(
        x_sample, state_lru_h[0], cache_lru_conv[0], cache_ccm_conv[0],
        mixer_params, ffn_params)
    return (y_p, y_s, p_h, p_lb, p_cb, s_h, s_lb, s_cb)
```

```python
import functools

import jax
import jax.numpy as jnp
from jax import lax
from jax.experimental import pallas as pl
from jax.experimental.pallas import tpu as pltpu

D_MODEL = 1024
D_LRU = 1024
D_CONV = 1024
LRU_HEADS = 16
LRU_HEAD_DIM = D_LRU // LRU_HEADS
LRU_CONV = 4
CCM_KERNEL = 31
RG_C = 8.0
EPS = 1e-6
N_SPLITS = 6

V7X_MXU_DIM = 256
V7X_LANES = 128
V7X_SUBLANES = 8
V7X_VMEM_BYTES = 64 * 1024 * 1024
GATE_BLOCK = V7X_MXU_DIM
LANE_BLOCKS = D_MODEL // V7X_LANES
ROW_TILE = 256

_F32 = jnp.float32
_BF16 = jnp.bfloat16


def _dot(a, w_packed):
    return jnp.dot(a, pltpu.bitcast(w_packed, _BF16), preferred_element_type=_F32)


def _sigmoid(x):
    return 0.5 * jnp.tanh(0.5 * x) + 0.5


def _rms_norm(x, g):
    return x * lax.rsqrt(jnp.mean(x * x, axis=-1, keepdims=True) + EPS) * g


def _lane_block(j):
    return slice(j * V7X_LANES, (j + 1) * V7X_LANES)


def _mixer_kernel(x_ref, h0_ref, lb0_ref, cb0_ref,
                  g_mix_ref, w_in_ref, b_in_ref, w_lc_ref, b_lc_ref,
                  w_r_ref, b_r_ref, w_i_ref, b_i_ref, lam_ref, w_lo_ref,
                  w_dw_ref, b_dw_ref, g_ln_ref, b_ln_ref, w_co_ref, w_out_ref,
                  x1_ref, h_out_ref, lb_out_ref, cb_out_ref,
                  lru_buf, ccm_buf, h_state, perm_buf, *, nb, tt):
    rows = nb * tt
    lru_hist = (LRU_CONV - 1) * nb
    ccm_hist = (CCM_KERNEL - 1) * nb

    @pl.when(pl.program_id(0) == 0)
    def _load_carried_state():
        lru_buf[0:lru_hist, :] = lb0_ref[...]
        ccm_buf[0:ccm_hist, :] = cb0_ref[...]
        h_state[...] = h0_ref[...]

    for b in range(nb):
        xb = x_ref[b]
        for j in range(LANE_BLOCKS):
            perm_buf[j, pl.ds(b, tt, stride=nb), :] = xb[:, _lane_block(j)]
    x = jnp.concatenate([perm_buf[j] for j in range(LANE_BLOCKS)], axis=-1)
    h = _rms_norm(x, g_mix_ref[...]).astype(_BF16)

    def proj(j):
        cols = slice(j * D_MODEL, (j + 1) * D_MODEL)
        return _dot(h, w_in_ref[:, cols]) + b_in_ref[:, cols]

    lru_buf[lru_hist:lru_hist + rows, :] = proj(0)
    xc = b_lc_ref[...] + w_lc_ref[0:1, :] * lru_buf[0:rows, :]
    for k in range(1, LRU_CONV):
        xc = xc + w_lc_ref[k:k + 1, :] * lru_buf[k * nb:k * nb + rows, :]
    xc_b = xc.astype(_BF16)

    def block_diag_dot(w_ref):
        parts = []
        for g in range(D_LRU // GATE_BLOCK):
            cols = slice(g * GATE_BLOCK, (g + 1) * GATE_BLOCK)
            parts.append(_dot(xc_b[:, cols], w_ref[g]))
        return jnp.concatenate(parts, axis=-1)

    r = _sigmoid(block_diag_dot(w_r_ref) + b_r_ref[...])
    i = _sigmoid(block_diag_dot(w_i_ref) + b_i_ref[...])
    neg_lam = -lam_ref[...]
    softplus = jnp.maximum(neg_lam, 0.0) + jnp.log1p(jnp.exp(-jnp.abs(neg_lam)))
    a = jnp.exp((-RG_C * softplus) * r)
    u = jnp.sqrt(1.0 - a * a) * (i * xc)

    carry = h_state[...]
    frames = []
    for t in range(tt):
        carry = a[t * nb:(t + 1) * nb, :] * carry + u[t * nb:(t + 1) * nb, :]
        frames.append(carry)
    h_state[...] = carry
    hs = jnp.concatenate(frames, axis=0)

    gated = (jax.nn.gelu(proj(1)) * hs).astype(_BF16)
    lru_out = _dot(gated, w_lo_ref[...])

    ccm_buf[ccm_hist:ccm_hist + rows, :] = proj(2) * _sigmoid(proj(3))
    d = b_dw_ref[...] + w_dw_ref[0:1, :] * ccm_buf[0:rows, :]
    for k in range(1, CCM_KERNEL):
        d = d + w_dw_ref[k:k + 1, :] * ccm_buf[k * nb:k * nb + rows, :]
    mu = jnp.mean(d, axis=-1, keepdims=True)
    dc = d - mu
    var = jnp.mean(dc * dc, axis=-1, keepdims=True)
    dn = dc * lax.rsqrt(var + EPS) * g_ln_ref[...] + b_ln_ref[...]
    dn = dn * _sigmoid(dn)
    ccm_out = _dot(dn.astype(_BF16), w_co_ref[...])

    merged = _sigmoid(proj(4)) * lru_out + _sigmoid(proj(5)) * ccm_out
    delta = _dot(merged.astype(_BF16), w_out_ref[...])

    for j in range(LANE_BLOCKS):
        perm_buf[j] = delta[:, _lane_block(j)]
    for b in range(nb):
        delta_b = jnp.concatenate(
            [perm_buf[j, pl.ds(b, tt, stride=nb), :] for j in range(LANE_BLOCKS)], axis=-1)
        x1_ref[b] = x_ref[b] + delta_b

    lru_tail = lru_buf[rows:rows + lru_hist, :]
    ccm_tail = ccm_buf[rows:rows + ccm_hist, :]
    lru_buf[0:lru_hist, :] = lru_tail
    ccm_buf[0:ccm_hist, :] = ccm_tail
    lb_out_ref[...] = lru_tail
    cb_out_ref[...] = ccm_tail
    h_out_ref[...] = h_state[...]


def _ffn_kernel(x1_ref, g_ffn_ref, w_gate_ref, w_up_ref, w_down_ref, g_final_ref, y_ref):
    x1 = x1_ref[...]
    h2 = _rms_norm(x1, g_ffn_ref[...]).astype(_BF16)
    gate = _dot(h2, w_gate_ref[...])
    ff = (gate * _sigmoid(gate) * _dot(h2, w_up_ref[...])).astype(_BF16)
    x2 = x1 + _dot(ff, w_down_ref[...])
    y_ref[...] = _rms_norm(x2, g_final_ref[...])


def _resident(shape):
    return pl.BlockSpec(shape, lambda s: (0,) * len(shape), pipeline_mode=pl.Buffered(1))


def _nbytes(shape, dtype):
    n = jnp.dtype(dtype).itemsize
    for s in shape:
        n *= s
    return n


def _vmem_limit(resident_bytes, tile_bytes, scratch_bytes, live_tiles):
    need = resident_bytes + 2 * tile_bytes + scratch_bytes + live_tiles
    return min(int(need), V7X_VMEM_BYTES)


def _mixer_call(x, h0, lb0, cb0, params, *, tt):
    nb, seq, _ = x.shape
    rows = nb * tt
    lru_hist = (LRU_CONV - 1) * nb
    ccm_hist = (CCM_KERNEL - 1) * nb
    time_tile = pl.BlockSpec((nb, tt, D_MODEL), lambda s: (0, s, 0))
    resident = [h0, lb0, cb0] + list(params)
    scratch = [
        ((lru_hist + rows, D_LRU), _F32),
        ((ccm_hist + rows, D_CONV), _F32),
        ((nb, D_LRU), _F32),
        ((LANE_BLOCKS, rows, V7X_LANES), _F32),
    ]
    state_shapes = [(nb, D_LRU), (lru_hist, D_LRU), (ccm_hist, D_CONV)]
    limit = _vmem_limit(
        sum(_nbytes(a.shape, a.dtype) for a in resident)
        + sum(_nbytes(s, _F32) for s in state_shapes),
        2 * _nbytes((rows, D_MODEL), _F32),
        sum(_nbytes(s, d) for s, d in scratch),
        16 * _nbytes((rows, D_MODEL), _F32))
    return pl.pallas_call(
        functools.partial(_mixer_kernel, nb=nb, tt=tt),
        grid=(seq // tt,),
        in_specs=[time_tile] + [_resident(a.shape) for a in resident],
        out_specs=[time_tile] + [pl.BlockSpec(s, lambda i: (0, 0)) for s in state_shapes],
        out_shape=[jax.ShapeDtypeStruct(s, _F32) for s in [x.shape] + state_shapes],
        scratch_shapes=[pltpu.VMEM(s, d) for s, d in scratch],
        compiler_params=pltpu.CompilerParams(
            dimension_semantics=("arbitrary",), vmem_limit_bytes=limit),
        name="mixer",
    )(x, *resident)


def _ffn_call(x1, params):
    n_rows = x1.shape[0]
    d_ff = params[1].shape[1]
    row_tile = pl.BlockSpec((ROW_TILE, D_MODEL), lambda s: (s, 0))
    limit = _vmem_limit(
        sum(_nbytes(a.shape, a.dtype) for a in params),
        2 * _nbytes((ROW_TILE, D_MODEL), _F32),
        0,
        6 * _nbytes((ROW_TILE, d_ff), _F32))
    return pl.pallas_call(
        _ffn_kernel,
        grid=(n_rows // ROW_TILE,),
        in_specs=[row_tile] + [_resident(a.shape) for a in params],
        out_specs=row_tile,
        out_shape=jax.ShapeDtypeStruct((n_rows, D_MODEL), _F32),
        compiler_params=pltpu.CompilerParams(
            dimension_semantics=("arbitrary",), vmem_limit_bytes=limit),
        name="ffn",
    )(x1, *params)


def _pack_rows(w):
    *lead, k, n = w.shape
    pairs = jnp.swapaxes(w.astype(_BF16).reshape(*lead, k // 2, 2, n), -1, -2)
    return lax.bitcast_convert_type(pairs, jnp.uint32)


def _pack_gate_heads(w):
    per = GATE_BLOCK // LRU_HEAD_DIM
    w = w.reshape(LRU_HEADS // per, per, LRU_HEAD_DIM, LRU_HEAD_DIM)
    eye = jnp.eye(per, dtype=w.dtype)
    blocks = jnp.einsum("gpde,pq->gpdqe", w, eye)
    return _pack_rows(blocks.reshape(LRU_HEADS // per, GATE_BLOCK, GATE_BLOCK))


def _time_major(a):
    b, t, c = a.shape
    return jnp.swapaxes(a, 0, 1).reshape(t * b, c)


def _stream_major(a, nb):
    return jnp.swapaxes(a.reshape(-1, nb, a.shape[-1]), 0, 1)


def _trunk(x, h0, lru_cache, ccm_cache, mixer_params, ffn_params):
    nb, seq, _ = x.shape
    assert ROW_TILE % nb == 0 and nb % V7X_SUBLANES == 0
    tt = ROW_TILE // nb
    assert seq % tt == 0 and tt % V7X_SUBLANES == 0
    x1, h_last, lru_tail, ccm_tail = _mixer_call(
        x, h0, _time_major(lru_cache), _time_major(ccm_cache), mixer_params, tt=tt)
    y = _ffn_call(x1.reshape(nb * seq, D_MODEL), ffn_params).reshape(x.shape)
    return (y, h_last[None], _stream_major(lru_tail, nb)[None],
            _stream_major(ccm_tail, nb)[None])


def kernel(x_prompt, x_sample, state_lru_h, cache_lru_conv, cache_ccm_conv, g_mix, w_in, b_in, w_lru_conv, b_lru_conv, w_rg_r, b_rg_r, w_rg_i, b_rg_i, lru_lambda, w_lru_o, w_ccm_dw, b_ccm_dw, g_ccm_ln, b_ccm_ln, w_ccm_o, w_out, g_ffn, w_ffn_gate, w_ffn_up, w_ffn_down, g_final):
    assert g_mix.shape[0] == 1, "single-layer trunk"
    row = lambda v: v.reshape(1, -1)
    mixer_params = (
        row(g_mix[0]), _pack_rows(w_in[0]), row(b_in[0]), w_lru_conv[0], row(b_lru_conv[0]),
        _pack_gate_heads(w_rg_r[0]), row(b_rg_r[0]), _pack_gate_heads(w_rg_i[0]), row(b_rg_i[0]),
        row(lru_lambda[0]), _pack_rows(w_lru_o[0]),
        w_ccm_dw[0], row(b_ccm_dw[0]), row(g_ccm_ln[0]), row(b_ccm_ln[0]),
        _pack_rows(w_ccm_o[0]), _pack_rows(w_out[0]))
    ffn_params = (
        row(g_ffn[0]), _pack_rows(w_ffn_gate[0]), _pack_rows(w_ffn_up[0]),
        _pack_rows(w_ffn_down[0]), row(g_final))

    bp = x_prompt.shape[0]
    dt = x_prompt.dtype
    y_p, p_h, p_lb, p_cb = _trunk(
        x_prompt, jnp.zeros((bp, D_LRU), dt),
        jnp.zeros((bp, LRU_CONV - 1, D_LRU), dt), jnp.zeros((bp, CCM_KERNEL - 1, D_CONV), dt),
        mixer_params, ffn_params)
    y_s, s_h, s_lb, s_cb = _trunk(
        x_sample, state_lru_h[0], cache_lru_conv[0], cache_ccm_conv[0],
        mixer_params, ffn_params)
    return (y_p, y_s, p_h, p_lb, p_cb, s_h, s_lb, s_cb)
```

```python
import functools

import jax
import jax.numpy as jnp
from jax import lax
from jax.experimental import pallas as pl
from jax.experimental.pallas import tpu as pltpu

D_MODEL = 1024
D_LRU = 1024
D_CONV = 1024
LRU_HEADS = 16
LRU_HEAD_DIM = D_LRU // LRU_HEADS
LRU_CONV = 4
CCM_KERNEL = 31
RG_C = 8.0
EPS = 1e-6

V7X_MXU_DIM = 256
V7X_LANES = 128
V7X_SUBLANES = 8
V7X_VMEM_BYTES = 64 * 1024 * 1024
GATE_BLOCK = V7X_MXU_DIM
LANE_BLOCKS = D_MODEL // V7X_LANES
ROW_TILE = 256

_F32 = jnp.float32
_BF16 = jnp.bfloat16


def _dot(a, w_packed):
    return jnp.dot(a, pltpu.bitcast(w_packed, _BF16), preferred_element_type=_F32)


def _sigmoid(x):
    return 0.5 * jnp.tanh(0.5 * x) + 0.5


def _rms_norm(x, g):
    return x * lax.rsqrt(jnp.mean(x * x, axis=-1, keepdims=True) + EPS) * g


def _lane_block(j):
    return slice(j * V7X_LANES, (j + 1) * V7X_LANES)


def _mixer_kernel(x_ref, h0_ref, lb0_ref, cb0_ref,
                  g_mix_ref, w_in_ref, b_in_ref, w_lc_ref, b_lc_ref,
                  w_r_ref, b_r_ref, w_i_ref, b_i_ref, lam_ref, w_lo_ref,
                  w_dw_ref, b_dw_ref, g_ln_ref, b_ln_ref, w_co_ref, w_out_ref,
                  x1_ref, h_out_ref, lb_out_ref, cb_out_ref,
                  lru_buf, ccm_buf, h_state, perm_buf, *, nb, tt):
    rows = nb * tt
    lru_hist = (LRU_CONV - 1) * nb
    ccm_hist = (CCM_KERNEL - 1) * nb

    @pl.when(pl.program_id(0) == 0)
    def _load_carried_state():
        lru_buf[0:lru_hist, :] = lb0_ref[...]
        ccm_buf[0:ccm_hist, :] = cb0_ref[...]
        h_state[...] = h0_ref[...]

    for b in range(nb):
        xb = x_ref[b]
        for j in range(LANE_BLOCKS):
            perm_buf[j, pl.ds(b, tt, stride=nb), :] = xb[:, _lane_block(j)]
    x = jnp.concatenate([perm_buf[j] for j in range(LANE_BLOCKS)], axis=-1)
    h = _rms_norm(x, g_mix_ref[...]).astype(_BF16)

    def proj(j):
        cols = slice(j * D_MODEL, (j + 1) * D_MODEL)
        return _dot(h, w_in_ref[:, cols]) + b_in_ref[:, cols]

    lru_buf[lru_hist:lru_hist + rows, :] = proj(0)
    xc = b_lc_ref[...] + w_lc_ref[0:1, :] * lru_buf[0:rows, :]
    for k in range(1, LRU_CONV):
        xc = xc + w_lc_ref[k:k + 1, :] * lru_buf[k * nb:k * nb + rows, :]
    xc_b = xc.astype(_BF16)

    def block_diag_dot(w_ref):
        parts = []
        for g in range(D_LRU // GATE_BLOCK):
            cols = slice(g * GATE_BLOCK, (g + 1) * GATE_BLOCK)
            parts.append(_dot(xc_b[:, cols], w_ref[g]))
        return jnp.concatenate(parts, axis=-1)

    r = _sigmoid(block_diag_dot(w_r_ref) + b_r_ref[...])
    i = _sigmoid(block_diag_dot(w_i_ref) + b_i_ref[...])
    neg_lam = -lam_ref[...]
    softplus = jnp.maximum(neg_lam, 0.0) + jnp.log1p(jnp.exp(-jnp.abs(neg_lam)))
    a = jnp.exp((-RG_C * softplus) * r)
    u = jnp.sqrt(1.0 - a * a) * (i * xc)

    carry = h_state[...]
    frames = []
    for t in range(tt):
        carry = a[t * nb:(t + 1) * nb, :] * carry + u[t * nb:(t + 1) * nb, :]
        frames.append(carry)
    h_state[...] = carry
    hs = jnp.concatenate(frames, axis=0)

    gated = (jax.nn.gelu(proj(1)) * hs).astype(_BF16)
    lru_out = _dot(gated, w_lo_ref[...])

    ccm_buf[ccm_hist:ccm_hist + rows, :] = proj(2) * _sigmoid(proj(3))
    d = b_dw_ref[...] + w_dw_ref[0:1, :] * ccm_buf[0:rows, :]
    for k in range(1, CCM_KERNEL):
        d = d + w_dw_ref[k:k + 1, :] * ccm_buf[k * nb:k * nb + rows, :]
    mu = jnp.mean(d, axis=-1, keepdims=True)
    dc = d - mu
    var = jnp.mean(dc * dc, axis=-1, keepdims=True)
    dn = dc * lax.rsqrt(var + EPS) * g_ln_ref[...] + b_ln_ref[...]
    dn = dn * _sigmoid(dn)
    ccm_out = _dot(dn.astype(_BF16), w_co_ref[...])

    merged = _sigmoid(proj(4)) * lru_out + _sigmoid(proj(5)) * ccm_out
    delta = _dot(merged.astype(_BF16), w_out_ref[...])

    for j in range(LANE_BLOCKS):
        perm_buf[j] = delta[:, _lane_block(j)]
    for b in range(nb):
        delta_b = jnp.concatenate(
            [perm_buf[j, pl.ds(b, tt, stride=nb), :] for j in range(LANE_BLOCKS)], axis=-1)
        x1_ref[b] = x_ref[b] + delta_b

    lru_tail = lru_buf[rows:rows + lru_hist, :]
    ccm_tail = ccm_buf[rows:rows + ccm_hist, :]
    lru_buf[0:lru_hist, :] = lru_tail
    ccm_buf[0:ccm_hist, :] = ccm_tail
    lb_out_ref[...] = lru_tail
    cb_out_ref[...] = ccm_tail
    h_out_ref[...] = h_state[...]


def _ffn_kernel(x1_ref, g_ffn_ref, w_gate_ref, w_up_ref, w_down_ref, g_final_ref, y_ref):
    x1 = x1_ref[...]
    h2 = _rms_norm(x1, g_ffn_ref[...]).astype(_BF16)
    gate = _dot(h2, w_gate_ref[...])
    ff = (gate * _sigmoid(gate) * _dot(h2, w_up_ref[...])).astype(_BF16)
    x2 = x1 + _dot(ff, w_down_ref[...])
    y_ref[...] = _rms_norm(x2, g_final_ref[...])


def _resident(shape):
    return pl.BlockSpec(shape, lambda s: (0,) * len(shape), pipeline_mode=pl.Buffered(1))


def _nbytes(shape, dtype):
    n = jnp.dtype(dtype).itemsize
    for s in shape:
        n *= s
    return n


def _vmem_limit(resident_bytes, tile_bytes, scratch_bytes, live_tiles):
    need = resident_bytes + 2 * tile_bytes + scratch_bytes + live_tiles
    return min(int(need), V7X_VMEM_BYTES)


def _mixer_call(x, h0, lb0, cb0, params, *, tt):
    nb, seq, _ = x.shape
    rows = nb * tt
    lru_hist = (LRU_CONV - 1) * nb
    ccm_hist = (CCM_KERNEL - 1) * nb
    time_tile = pl.BlockSpec((nb, tt, D_MODEL), lambda s: (0, s, 0))
    resident = [h0, lb0, cb0] + list(params)
    scratch = [
        ((lru_hist + rows, D_LRU), _F32),
        ((ccm_hist + rows, D_CONV), _F32),
        ((nb, D_LRU), _F32),
        ((LANE_BLOCKS, rows, V7X_LANES), _F32),
    ]
    state_shapes = [(nb, D_LRU), (lru_hist, D_LRU), (ccm_hist, D_CONV)]
    limit = _vmem_limit(
        sum(_nbytes(a.shape, a.dtype) for a in resident)
        + sum(_nbytes(s, _F32) for s in state_shapes),
        2 * _nbytes((rows, D_MODEL), _F32),
        sum(_nbytes(s, d) for s, d in scratch),
        16 * _nbytes((rows, D_MODEL), _F32))
    return pl.pallas_call(
        functools.partial(_mixer_kernel, nb=nb, tt=tt),
        grid=(seq // tt,),
        in_specs=[time_tile] + [_resident(a.shape) for a in resident],
        out_specs=[time_tile] + [pl.BlockSpec(s, lambda i: (0, 0)) for s in state_shapes],
        out_shape=[jax.ShapeDtypeStruct(s, _F32) for s in [x.shape] + state_shapes],
        scratch_shapes=[pltpu.VMEM(s, d) for s, d in scratch],
        compiler_params=pltpu.CompilerParams(
            dimension_semantics=("arbitrary",), vmem_limit_bytes=limit),
        name="mixer",
    )(x, *resident)


def _ffn_call(x1, params):
    n_rows = x1.shape[0]
    d_ff = params[1].shape[1]
    row_tile = pl.BlockSpec((ROW_TILE, D_MODEL), lambda s: (s, 0))
    limit = _vmem_limit(
        sum(_nbytes(a.shape, a.dtype) for a in params),
        2 * _nbytes((ROW_TILE, D_MODEL), _F32),
        0,
        6 * _nbytes((ROW_TILE, d_ff), _F32))
    return pl.pallas_call(
        _ffn_kernel,
        grid=(n_rows // ROW_TILE,),
        in_specs=[row_tile] + [_resident(a.shape) for a in params],
        out_specs=row_tile,
        out_shape=jax.ShapeDtypeStruct((n_rows, D_MODEL), _F32),
        compiler_params=pltpu.CompilerParams(
            dimension_semantics=("arbitrary",), vmem_limit_bytes=limit),
        name="ffn",
    )(x1, *params)


def _pack_rows(w):
    bits = lax.bitcast_convert_type(w.astype(_BF16), jnp.uint16).astype(jnp.uint32)
    return bits[..., 0::2, :] | (bits[..., 1::2, :] << 16)


def _pack_gate_heads(w):
    per = GATE_BLOCK // LRU_HEAD_DIM
    w = w.reshape(LRU_HEADS // per, per, LRU_HEAD_DIM, LRU_HEAD_DIM)
    eye = jnp.eye(per, dtype=w.dtype)
    blocks = jnp.einsum("gpde,pq->gpdqe", w, eye)
    return _pack_rows(blocks.reshape(LRU_HEADS // per, GATE_BLOCK, GATE_BLOCK))


def _time_major(a):
    b, t, c = a.shape
    return jnp.swapaxes(a, 0, 1).reshape(t * b, c)


def _stream_major(a, nb):
    return jnp.swapaxes(a.reshape(-1, nb, a.shape[-1]), 0, 1)


def _trunk(x, h0, lru_cache, ccm_cache, mixer_params, ffn_params):
    nb, seq, _ = x.shape
    assert ROW_TILE % nb == 0 and nb % V7X_SUBLANES == 0
    tt = ROW_TILE // nb
    assert seq % tt == 0 and tt % V7X_SUBLANES == 0
    x1, h_last, lru_tail, ccm_tail = _mixer_call(
        x, h0, _time_major(lru_cache), _time_major(ccm_cache), mixer_params, tt=tt)
    y = _ffn_call(x1.reshape(nb * seq, D_MODEL), ffn_params).reshape(x.shape)
    return (y, h_last[None], _stream_major(lru_tail, nb)[None],
            _stream_major(ccm_tail, nb)[None])


def kernel(x_prompt, x_sample, state_lru_h, cache_lru_conv, cache_ccm_conv, g_mix, w_in, b_in, w_lru_conv, b_lru_conv, w_rg_r, b_rg_r, w_rg_i, b_rg_i, lru_lambda, w_lru_o, w_ccm_dw, b_ccm_dw, g_ccm_ln, b_ccm_ln, w_ccm_o, w_out, g_ffn, w_ffn_gate, w_ffn_up, w_ffn_down, g_final):
    assert g_mix.shape[0] == 1, "single-layer trunk"
    row = lambda v: v.reshape(1, -1)
    mixer_params = (
        row(g_mix[0]), _pack_rows(w_in[0]), row(b_in[0]), w_lru_conv[0], row(b_lru_conv[0]),
        _pack_gate_heads(w_rg_r[0]), row(b_rg_r[0]), _pack_gate_heads(w_rg_i[0]), row(b_rg_i[0]),
        row(lru_lambda[0]), _pack_rows(w_lru_o[0]),
        w_ccm_dw[0], row(b_ccm_dw[0]), row(g_ccm_ln[0]), row(b_ccm_ln[0]),
        _pack_rows(w_ccm_o[0]), _pack_rows(w_out[0]))
    ffn_params = (
        row(g_ffn[0]), _pack_rows(w_ffn_gate[0]), _pack_rows(w_ffn_up[0]),
        _pack_rows(w_ffn_down[0]), row(g_final))

    bp = x_prompt.shape[0]
    dt = x_prompt.dtype
    y_p, p_h, p_lb, p_cb = _trunk(
        x_prompt, jnp.zeros((bp, D_LRU), dt),
        jnp.zeros((bp, LRU_CONV - 1, D_LRU), dt), jnp.zeros((bp, CCM_KERNEL - 1, D_CONV), dt),
        mixer_params, ffn_params)
    y_s, s_h, s_lb, s_cb = _trunk(
        x_sample, state_lru_h[0], cache_lru_conv[0], cache_ccm_conv[0],
        mixer_params, ffn_params)
    return (y_p, y_s, p_h, p_lb, p_cb, s_h, s_lb, s_cb)
```

```python
import functools

import jax
import jax.numpy as jnp
from jax import lax
from jax.experimental import pallas as pl
from jax.experimental.pallas import tpu as pltpu

D_MODEL = 1024
D_LRU = 1024
D_CONV = 1024
LRU_HEADS = 16
LRU_HEAD_DIM = D_LRU // LRU_HEADS
LRU_CONV = 4
CCM_KERNEL = 31
RG_C = 8.0
EPS = 1e-6

V7X_MXU_DIM = 256
V7X_LANES = 128
V7X_SUBLANES = 8
V7X_VMEM_BYTES = 64 * 1024 * 1024
GATE_BLOCK = V7X_MXU_DIM
LANE_BLOCKS = D_MODEL // V7X_LANES
ROW_TILE = 256
PACK_ROWS = 128

_F32 = jnp.float32
_BF16 = jnp.bfloat16


def _dot(a, w_packed):
    return jnp.dot(a, pltpu.bitcast(w_packed, _BF16), preferred_element_type=_F32)


def _sigmoid(x):
    return 0.5 * jnp.tanh(0.5 * x) + 0.5


def _rms_norm(x, g):
    return x * lax.rsqrt(jnp.mean(x * x, axis=-1, keepdims=True) + EPS) * g


def _lane_block(j):
    return slice(j * V7X_LANES, (j + 1) * V7X_LANES)


def _zero_after(value):
    bits = value[0:1, 0:V7X_LANES].astype(jnp.int32)
    zero = lax.shift_right_logical(lax.shift_right_logical(bits, 16), 16)
    return zero.astype(_F32)


def _mxu_row(zero_row):
    return jnp.concatenate([zero_row] * (V7X_MXU_DIM // V7X_LANES), axis=-1)


def _mixer_kernel(x_ref, h0_ref, lb0_ref, cb0_ref,
                  g_mix_ref, w_in_ref, b_in_ref, w_lc_ref, b_lc_ref,
                  w_r_ref, b_r_ref, w_i_ref, b_i_ref, lam_ref, w_lo_ref,
                  w_dw_ref, b_dw_ref, g_ln_ref, b_ln_ref, w_co_ref, w_out_ref,
                  x1_ref, h_out_ref, lb_out_ref, cb_out_ref,
                  lru_buf, ccm_buf, h_state, perm_buf, *, nb, tt):
    rows = nb * tt
    lru_hist = (LRU_CONV - 1) * nb
    ccm_hist = (CCM_KERNEL - 1) * nb

    @pl.when(pl.program_id(0) == 0)
    def _load_carried_state():
        lru_buf[0:lru_hist, :] = lb0_ref[...]
        ccm_buf[0:ccm_hist, :] = cb0_ref[...]
        h_state[...] = h0_ref[...]

    for b in range(nb):
        xb = x_ref[b]
        for j in range(LANE_BLOCKS):
            perm_buf[j, pl.ds(b, tt, stride=nb), :] = xb[:, _lane_block(j)]
    x = jnp.concatenate([perm_buf[j] for j in range(LANE_BLOCKS)], axis=-1)
    h = _rms_norm(x, g_mix_ref[...]).astype(_BF16)

    def mxu_cols(n):
        return slice(n * V7X_MXU_DIM, (n + 1) * V7X_MXU_DIM)

    def proj_block(split, n, after):
        c0 = split * D_MODEL + n * V7X_MXU_DIM
        cols = slice(c0, c0 + V7X_MXU_DIM)
        return _dot(h, w_in_ref[:, cols]) + (b_in_ref[:, cols] + _mxu_row(after))

    def proj(split):
        cols = slice(split * D_MODEL, (split + 1) * D_MODEL)
        return _dot(h, w_in_ref[:, cols]) + b_in_ref[:, cols]

    lru_buf[lru_hist:lru_hist + rows, :] = proj(0)
    xc = b_lc_ref[...] + w_lc_ref[0:1, :] * lru_buf[0:rows, :]
    for k in range(1, LRU_CONV):
        xc = xc + w_lc_ref[k:k + 1, :] * lru_buf[k * nb:k * nb + rows, :]
    xc_b = xc.astype(_BF16)

    def block_diag_dot(w_ref):
        return jnp.concatenate(
            [_dot(xc_b[:, mxu_cols(g)], w_ref[g]) for g in range(D_LRU // GATE_BLOCK)], axis=-1)

    r = _sigmoid(block_diag_dot(w_r_ref) + b_r_ref[...])
    i = _sigmoid(block_diag_dot(w_i_ref) + b_i_ref[...])
    neg_lam = -lam_ref[...]
    softplus = jnp.maximum(neg_lam, 0.0) + jnp.log1p(jnp.exp(-jnp.abs(neg_lam)))
    a = jnp.exp((-RG_C * softplus) * r)
    u = jnp.sqrt(1.0 - a * a) * (i * xc)

    carry = h_state[...]
    frames = []
    for t in range(tt):
        carry = a[t * nb:(t + 1) * nb, :] * carry + u[t * nb:(t + 1) * nb, :]
        frames.append(carry)
    h_state[...] = carry
    hs = jnp.concatenate(frames, axis=0)

    ccm_buf[ccm_hist:ccm_hist + rows, :] = proj(2) * _sigmoid(proj(3))

    gated_blocks, lru_out_blocks, s_lru_blocks, s_ccm_blocks = [], [], [], []

    def gate_branch_a(n, after):
        gl = proj_block(1, n, after)
        gated_blocks.append((jax.nn.gelu(gl) * hs[:, mxu_cols(n)]).astype(_BF16))
        return gl

    def project_branch_a(n, after):
        gated = jnp.concatenate(gated_blocks, axis=-1)
        lru_out_blocks.append(_dot(gated, w_lo_ref[:, mxu_cols(n)]) + _mxu_row(after))
        return lru_out_blocks[-1]

    def merge_gate_a(n, after):
        s_lru_blocks.append(_sigmoid(proj_block(4, n, after)))
        return s_lru_blocks[-1]

    def merge_gate_b(n, after):
        s_ccm_blocks.append(_sigmoid(proj_block(5, n, after)))
        return s_ccm_blocks[-1]

    side_work = [functools.partial(f, n)
                 for f in (gate_branch_a, project_branch_a, merge_gate_a, merge_gate_b)
                 for n in range(D_MODEL // V7X_MXU_DIM)]
    per_lane_block = len(side_work) // LANE_BLOCKS

    d_blocks = []
    conv_done = _zero_after(ccm_buf[0:V7X_SUBLANES, :])
    side_done = conv_done
    for j in range(LANE_BLOCKS):
        cols = _lane_block(j)
        dj = (b_dw_ref[:, cols] + side_done) + w_dw_ref[0:1, cols] * ccm_buf[0:rows, cols]
        for k in range(1, CCM_KERNEL):
            dj = dj + w_dw_ref[k:k + 1, cols] * ccm_buf[k * nb:k * nb + rows, cols]
        d_blocks.append(dj)
        for work in side_work[j * per_lane_block:(j + 1) * per_lane_block]:
            last = work(conv_done)
        conv_done = _zero_after(dj[rows - V7X_SUBLANES:rows, :])
        side_done = _zero_after(last[rows - V7X_SUBLANES:rows, :])
    d = jnp.concatenate(d_blocks, axis=-1)
    lru_out = jnp.concatenate(lru_out_blocks, axis=-1)

    mu = jnp.mean(d, axis=-1, keepdims=True)
    dc = d - mu
    var = jnp.mean(dc * dc, axis=-1, keepdims=True)
    dn = dc * lax.rsqrt(var + EPS) * g_ln_ref[...] + b_ln_ref[...]
    dn = dn * _sigmoid(dn)
    ccm_out = _dot(dn.astype(_BF16), w_co_ref[...])

    merged = (jnp.concatenate(s_lru_blocks, axis=-1) * lru_out
              + jnp.concatenate(s_ccm_blocks, axis=-1) * ccm_out)
    delta = _dot(merged.astype(_BF16), w_out_ref[...])

    for j in range(LANE_BLOCKS):
        perm_buf[j] = delta[:, _lane_block(j)]
    for b in range(nb):
        delta_b = jnp.concatenate(
            [perm_buf[j, pl.ds(b, tt, stride=nb), :] for j in range(LANE_BLOCKS)], axis=-1)
        x1_ref[b] = x_ref[b] + delta_b

    lru_tail = lru_buf[rows:rows + lru_hist, :]
    ccm_tail = ccm_buf[rows:rows + ccm_hist, :]
    lru_buf[0:lru_hist, :] = lru_tail
    ccm_buf[0:ccm_hist, :] = ccm_tail
    lb_out_ref[...] = lru_tail
    cb_out_ref[...] = ccm_tail
    h_out_ref[...] = h_state[...]


def _ffn_kernel(x1_ref, g_ffn_ref, w_gate_ref, w_up_ref, w_down_ref, g_final_ref, y_ref):
    x1 = x1_ref[...]
    h2 = _rms_norm(x1, g_ffn_ref[...]).astype(_BF16)
    gate = _dot(h2, w_gate_ref[...])
    ff = (gate * _sigmoid(gate) * _dot(h2, w_up_ref[...])).astype(_BF16)
    x2 = x1 + _dot(ff, w_down_ref[...])
    y_ref[...] = _rms_norm(x2, g_final_ref[...])


def _resident(shape):
    return pl.BlockSpec(shape, lambda s: (0,) * len(shape), pipeline_mode=pl.Buffered(1))


def _nbytes(shape, dtype):
    n = jnp.dtype(dtype).itemsize
    for s in shape:
        n *= s
    return n


def _vmem_limit(resident_bytes, tile_bytes, scratch_bytes, live_tiles):
    need = resident_bytes + 2 * tile_bytes + scratch_bytes + live_tiles
    return min(int(need), V7X_VMEM_BYTES)


def _mixer_call(x, h0, lb0, cb0, params, *, tt):
    nb, seq, _ = x.shape
    rows = nb * tt
    lru_hist = (LRU_CONV - 1) * nb
    ccm_hist = (CCM_KERNEL - 1) * nb
    time_tile = pl.BlockSpec((nb, tt, D_MODEL), lambda s: (0, s, 0))
    resident = [h0, lb0, cb0] + list(params)
    scratch = [
        ((lru_hist + rows, D_LRU), _F32),
        ((ccm_hist + rows, D_CONV), _F32),
        ((nb, D_LRU), _F32),
        ((LANE_BLOCKS, rows, V7X_LANES), _F32),
    ]
    state_shapes = [(nb, D_LRU), (lru_hist, D_LRU), (ccm_hist, D_CONV)]
    limit = _vmem_limit(
        sum(_nbytes(a.shape, a.dtype) for a in resident)
        + sum(_nbytes(s, _F32) for s in state_shapes),
        2 * _nbytes((rows, D_MODEL), _F32),
        sum(_nbytes(s, d) for s, d in scratch),
        16 * _nbytes((rows, D_MODEL), _F32))
    return pl.pallas_call(
        functools.partial(_mixer_kernel, nb=nb, tt=tt),
        grid=(seq // tt,),
        in_specs=[time_tile] + [_resident(a.shape) for a in resident],
        out_specs=[time_tile] + [pl.BlockSpec(s, lambda i: (0, 0)) for s in state_shapes],
        out_shape=[jax.ShapeDtypeStruct(s, _F32) for s in [x.shape] + state_shapes],
        scratch_shapes=[pltpu.VMEM(s, d) for s, d in scratch],
        compiler_params=pltpu.CompilerParams(
            dimension_semantics=("arbitrary",), vmem_limit_bytes=limit),
        name="mixer",
    )(x, *resident)


def _ffn_call(x1, params):
    n_rows = x1.shape[0]
    d_ff = params[1].shape[1]
    row_tile = pl.BlockSpec((ROW_TILE, D_MODEL), lambda s: (s, 0))
    limit = _vmem_limit(
        sum(_nbytes(a.shape, a.dtype) for a in params),
        2 * _nbytes((ROW_TILE, D_MODEL), _F32),
        0,
        6 * _nbytes((ROW_TILE, d_ff), _F32))
    return pl.pallas_call(
        _ffn_kernel,
        grid=(n_rows // ROW_TILE,),
        in_specs=[row_tile] + [_resident(a.shape) for a in params],
        out_specs=row_tile,
        out_shape=jax.ShapeDtypeStruct((n_rows, D_MODEL), _F32),
        compiler_params=pltpu.CompilerParams(
            dimension_semantics=("arbitrary",), vmem_limit_bytes=limit),
        name="ffn",
    )(x1, *params)


def _pack_kernel(w_ref, o_ref):
    o_ref[...] = pltpu.bitcast(w_ref[...].astype(_BF16), jnp.uint32)


def _pack_rows(w):
    k, n = w.shape
    rows = min(PACK_ROWS, k)
    return pl.pallas_call(
        _pack_kernel,
        grid=(k // rows,),
        in_specs=[pl.BlockSpec((rows, n), lambda s: (s, 0))],
        out_specs=pl.BlockSpec((rows // 2, n), lambda s: (s, 0)),
        out_shape=jax.ShapeDtypeStruct((k // 2, n), jnp.uint32),
        name="pack_rows",
    )(w)


def _pack_gate_heads(w):
    per = GATE_BLOCK // LRU_HEAD_DIM
    w = w.reshape(LRU_HEADS // per, per, LRU_HEAD_DIM, LRU_HEAD_DIM)
    eye = jnp.eye(per, dtype=w.dtype)
    blocks = jnp.einsum("gpde,pq->gpdqe", w, eye)
    packed = _pack_rows(blocks.reshape(D_LRU, GATE_BLOCK))
    return packed.reshape(D_LRU // GATE_BLOCK, GATE_BLOCK // 2, GATE_BLOCK)


def _time_major(a):
    b, t, c = a.shape
    return jnp.swapaxes(a, 0, 1).reshape(t * b, c)


def _stream_major(a, nb):
    return jnp.swapaxes(a.reshape(-1, nb, a.shape[-1]), 0, 1)


def _trunk(x, h0, lru_cache, ccm_cache, mixer_params, ffn_params):
    nb, seq, _ = x.shape
    assert ROW_TILE % nb == 0 and nb % V7X_SUBLANES == 0
    tt = ROW_TILE // nb
    assert seq % tt == 0 and tt % V7X_SUBLANES == 0
    x1, h_last, lru_tail, ccm_tail = _mixer_call(
        x, h0, _time_major(lru_cache), _time_major(ccm_cache), mixer_params, tt=tt)
    y = _ffn_call(x1.reshape(nb * seq, D_MODEL), ffn_params).reshape(x.shape)
    return (y, h_last[None], _stream_major(lru_tail, nb)[None],
            _stream_major(ccm_tail, nb)[None])


def kernel(x_prompt, x_sample, state_lru_h, cache_lru_conv, cache_ccm_conv, g_mix, w_in, b_in, w_lru_conv, b_lru_conv, w_rg_r, b_rg_r, w_rg_i, b_rg_i, lru_lambda, w_lru_o, w_ccm_dw, b_ccm_dw, g_ccm_ln, b_ccm_ln, w_ccm_o, w_out, g_ffn, w_ffn_gate, w_ffn_up, w_ffn_down, g_final):
    assert g_mix.shape[0] == 1, "single-layer trunk"
    row = lambda v: v.reshape(1, -1)
    mixer_params = (
        row(g_mix[0]), _pack_rows(w_in[0]), row(b_in[0]), w_lru_conv[0], row(b_lru_conv[0]),
        _pack_gate_heads(w_rg_r[0]), row(b_rg_r[0]), _pack_gate_heads(w_rg_i[0]), row(b_rg_i[0]),
        row(lru_lambda[0]), _pack_rows(w_lru_o[0]),
        w_ccm_dw[0], row(b_ccm_dw[0]), row(g_ccm_ln[0]), row(b_ccm_ln[0]),
        _pack_rows(w_ccm_o[0]), _pack_rows(w_out[0]))
    ffn_params = (
        row(g_ffn[0]), _pack_rows(w_ffn_gate[0]), _pack_rows(w_ffn_up[0]),
        _pack_rows(w_ffn_down[0]), row(g_final))

    bp = x_prompt.shape[0]
    dt = x_prompt.dtype
    y_p, p_h, p_lb, p_cb = _trunk(
        x_prompt, jnp.zeros((bp, D_LRU), dt),
        jnp.zeros((bp, LRU_CONV - 1, D_LRU), dt), jnp.zeros((bp, CCM_KERNEL - 1, D_CONV), dt),
        mixer_params, ffn_params)
    y_s, s_h, s_lb, s_cb = _trunk(
        x_sample, state_lru_h[0], cache_lru_conv[0], cache_ccm_conv[0],
        mixer_params, ffn_params)
    return (y_p, y_s, p_h, p_lb, p_cb, s_h, s_lb, s_cb)
```

```python
import functools

import jax
import jax.numpy as jnp
from jax import lax
from jax.experimental import pallas as pl
from jax.experimental.pallas import tpu as pltpu

D_MODEL = 1024
D_LRU = 1024
D_CONV = 1024
LRU_HEADS = 16
LRU_HEAD_DIM = D_LRU // LRU_HEADS
LRU_CONV = 4
CCM_KERNEL = 31
RG_C = 8.0
EPS = 1e-6

V7X_MXU_DIM = 256
V7X_LANES = 128
V7X_SUBLANES = 8
V7X_VMEM_BYTES = 64 * 1024 * 1024
GATE_BLOCK = V7X_MXU_DIM
LANE_BLOCKS = D_MODEL // V7X_LANES
MXU_BLOCKS = D_MODEL // V7X_MXU_DIM
GROUP_STREAMS = V7X_SUBLANES
ROW_TILE = 128
PACK_ROWS = 128

_F32 = jnp.float32
_BF16 = jnp.bfloat16


def _dot(a, w_packed):
    return jnp.dot(a, pltpu.bitcast(w_packed, _BF16), preferred_element_type=_F32)


def _sigmoid(x):
    return 0.5 * jnp.tanh(0.5 * x) + 0.5


def _rms_norm(x, g):
    return x * lax.rsqrt(jnp.mean(x * x, axis=-1, keepdims=True) + EPS) * g


def _lane_block(j):
    return slice(j * V7X_LANES, (j + 1) * V7X_LANES)


def _mxu_block(n):
    return slice(n * V7X_MXU_DIM, (n + 1) * V7X_MXU_DIM)


def _zero_after(value):
    bits = value[0:1, 0:V7X_LANES].astype(jnp.int32)
    zero = lax.shift_right_logical(lax.shift_right_logical(bits, 16), 16)
    return zero.astype(_F32)


def _widen(zero_row, width):
    return jnp.concatenate([zero_row] * (width // V7X_LANES), axis=-1)


class _Tiling:
    def __init__(self, prompt_seq, sample_streams, sample_seq):
        self.tt = ROW_TILE // GROUP_STREAMS
        assert prompt_seq % self.tt == 0 and sample_seq % self.tt == 0
        assert sample_streams % GROUP_STREAMS == 0
        self.n_prompt = prompt_seq // self.tt
        self.n_sample = sample_seq // self.tt
        self.sample_groups = sample_streams // GROUP_STREAMS
        self.n_tiles = self.n_prompt + self.sample_groups * self.n_sample
        self.n_groups = 1 + self.sample_groups

    def sample_index(self, q):
        return jnp.clip(q - self.n_prompt, 0, self.sample_groups * self.n_sample - 1)

    def group(self, q):
        q = jnp.clip(q, 0, self.n_tiles - 1)
        return jnp.where(q < self.n_prompt, 0, 1 + self.sample_index(q) // self.n_sample)

    def is_group_start(self, q):
        return (q == 0) | ((q >= self.n_prompt) & (q < self.n_tiles)
                           & (self.sample_index(q) % self.n_sample == 0))

    def is_group_end(self, q):
        return ((q == self.n_prompt - 1)
                | ((q >= self.n_prompt) & (q < self.n_tiles)
                   & (self.sample_index(q) % self.n_sample == self.n_sample - 1)))


def _layer_kernel(xp_ref, xs_ref, h0_ref, lb0_ref, cb0_ref,
                  g_mix_ref, w_in_ref, b_in_ref, w_lc_ref, b_lc_ref,
                  w_r_ref, b_r_ref, w_i_ref, b_i_ref, lam_ref, w_lo_ref,
                  w_dw_ref, b_dw_ref, g_ln_ref, b_ln_ref, w_co_ref, w_out_ref,
                  g_ffn_ref, w_gate_ref, w_up_ref, w_down_ref, g_final_ref,
                  yp_ref, ys_ref, h_out_ref, lb_out_ref, cb_out_ref,
                  lru_buf, ccm_buf, h_state, perm_buf, x1_buf, *, tiling):
    step = pl.program_id(0)
    nb, tt = GROUP_STREAMS, tiling.tt
    rows = nb * tt
    lru_hist = (LRU_CONV - 1) * nb
    ccm_hist = (CCM_KERNEL - 1) * nb
    d_ff = w_gate_ref.shape[1]
    ffn_blocks = d_ff // V7X_MXU_DIM

    @pl.when(step == 0)
    def _clear_ffn_input():
        x1_buf[...] = jnp.zeros_like(x1_buf)

    @pl.when(tiling.is_group_start(step))
    def _load_carried_state():
        lru_buf[0:lru_hist, :] = lb0_ref[...]
        for j in range(LANE_BLOCKS):
            ccm_buf[j, 0:ccm_hist, :] = cb0_ref[:, _lane_block(j)]
        h_state[...] = h0_ref[...]

    x1_prev = x1_buf[...]
    h2 = _rms_norm(x1_prev, g_ffn_ref[...]).astype(_BF16)
    ff_blocks, x2_blocks = [], []

    def ffn_gate_up(n, after):
        cols = _mxu_block(n)
        gate = _dot(h2, w_gate_ref[:, cols]) + _widen(after, V7X_MXU_DIM)
        up = _dot(h2, w_up_ref[:, cols])
        ff_blocks.append((gate * _sigmoid(gate) * up).astype(_BF16))
        return up

    def ffn_down(n, after):
        cols = _mxu_block(n)
        ff = jnp.concatenate(ff_blocks, axis=-1)
        down = _dot(ff, w_down_ref[:, cols]) + _widen(after, V7X_MXU_DIM)
        x2_blocks.append(x1_prev[:, cols] + down)
        return down

    ffn_work = ([functools.partial(ffn_gate_up, n) for n in range(ffn_blocks)]
                + [functools.partial(ffn_down, n) for n in range(MXU_BLOCKS)])

    def run_ffn(count, after):
        last = None
        for _ in range(count):
            last = ffn_work.pop(0)(after)
        return last

    no_wait = jnp.zeros((1, V7X_LANES), _F32)

    from_prompt = step < tiling.n_prompt
    for b in range(nb):
        xb = jnp.where(from_prompt, xp_ref[b], xs_ref[b])
        for j in range(LANE_BLOCKS):
            perm_buf[j, pl.ds(b, tt, stride=nb), :] = xb[:, _lane_block(j)]
    x = jnp.concatenate([perm_buf[j] for j in range(LANE_BLOCKS)], axis=-1)
    h = _rms_norm(x, g_mix_ref[...]).astype(_BF16)

    def proj(split):
        cols = slice(split * D_MODEL, (split + 1) * D_MODEL)
        return _dot(h, w_in_ref[:, cols]) + b_in_ref[:, cols]

    def proj_block(split, n, after):
        c0 = split * D_MODEL + n * V7X_MXU_DIM
        cols = slice(c0, c0 + V7X_MXU_DIM)
        return _dot(h, w_in_ref[:, cols]) + (b_in_ref[:, cols] + _widen(after, V7X_MXU_DIM))

    xl = proj(0)
    lru_buf[lru_hist:lru_hist + rows, :] = xl
    ffn_done = _zero_after(run_ffn(1, no_wait))
    xc = (b_lc_ref[...] + _widen(ffn_done, D_LRU)) + w_lc_ref[0:1, :] * lru_buf[0:rows, :]
    for k in range(1, LRU_CONV):
        xc = xc + w_lc_ref[k:k + 1, :] * lru_buf[k * nb:k * nb + rows, :]
    xc_b = xc.astype(_BF16)
    ffn_done = _zero_after(run_ffn(2, _zero_after(xl)))

    def block_diag_dot(w_ref):
        return jnp.concatenate(
            [_dot(xc_b[:, _mxu_block(g)], w_ref[g]) for g in range(D_LRU // GATE_BLOCK)], axis=-1)

    r = _sigmoid(block_diag_dot(w_r_ref) + b_r_ref[...])
    i = _sigmoid(block_diag_dot(w_i_ref) + (b_i_ref[...] + _widen(ffn_done, D_LRU)))
    neg_lam = -lam_ref[...]
    softplus = jnp.maximum(neg_lam, 0.0) + jnp.log1p(jnp.exp(-jnp.abs(neg_lam)))
    a = jnp.exp((-RG_C * softplus) * r)
    u = jnp.sqrt(1.0 - a * a) * (i * xc)
    ffn_done = _zero_after(run_ffn(2, _zero_after(xc)))

    carry = h_state[...]
    frames = []
    for t in range(tt):
        carry = a[t * nb:(t + 1) * nb, :] * carry + u[t * nb:(t + 1) * nb, :]
        frames.append(carry)
    h_state[...] = carry
    hs = jnp.concatenate(frames, axis=0)

    glu = ((_dot(h, w_in_ref[:, 2 * D_MODEL:3 * D_MODEL])
            + (b_in_ref[:, 2 * D_MODEL:3 * D_MODEL] + _widen(ffn_done, D_MODEL)))
           * _sigmoid(proj(3)))
    for j in range(LANE_BLOCKS):
        ccm_buf[j, ccm_hist:ccm_hist + rows, :] = glu[:, _lane_block(j)]

    gated_blocks, lru_out_blocks, s_lru_blocks, s_ccm_blocks = [], [], [], []

    def gate_branch_a(n, after):
        gl = proj_block(1, n, after)
        gated_blocks.append((jax.nn.gelu(gl) * hs[:, _mxu_block(n)]).astype(_BF16))
        return gl

    def project_branch_a(n, after):
        gated = jnp.concatenate(gated_blocks, axis=-1)
        lru_out_blocks.append(
            _dot(gated, w_lo_ref[:, _mxu_block(n)]) + _widen(after, V7X_MXU_DIM))
        return lru_out_blocks[-1]

    def merge_gate_a(n, after):
        s_lru_blocks.append(_sigmoid(proj_block(4, n, after)))
        return s_lru_blocks[-1]

    def merge_gate_b(n, after):
        s_ccm_blocks.append(_sigmoid(proj_block(5, n, after)))
        return s_ccm_blocks[-1]

    side_work = [functools.partial(f, n)
                 for f in (gate_branch_a, project_branch_a, merge_gate_a, merge_gate_b)
                 for n in range(MXU_BLOCKS)]
    side_per_block = len(side_work) // LANE_BLOCKS
    ffn_during_conv = len(ffn_work) - MXU_BLOCKS
    ffn_per_block = -(-ffn_during_conv // LANE_BLOCKS)

    d_blocks = []
    conv_done = _zero_after(glu)
    side_done = conv_done
    for j in range(LANE_BLOCKS):
        cols = _lane_block(j)
        dj = (b_dw_ref[:, cols] + side_done) + w_dw_ref[0:1, cols] * ccm_buf[j, 0:rows, :]
        for k in range(1, CCM_KERNEL):
            dj = dj + w_dw_ref[k:k + 1, cols] * ccm_buf[j, k * nb:k * nb + rows, :]
        d_blocks.append(dj)
        for work in side_work[j * side_per_block:(j + 1) * side_per_block]:
            last = work(conv_done)
        n_ffn = min(ffn_per_block, len(ffn_work) - MXU_BLOCKS)
        if n_ffn:
            last = run_ffn(n_ffn, conv_done)
        conv_done = _zero_after(
            jnp.max(dj.reshape(rows // V7X_SUBLANES, V7X_SUBLANES, V7X_LANES), axis=0))
        side_done = _zero_after(last[rows - V7X_SUBLANES:rows, :])
    d = jnp.concatenate(d_blocks, axis=-1)
    lru_out = jnp.concatenate(lru_out_blocks, axis=-1)

    ffn_done = _zero_after(run_ffn(MXU_BLOCKS // 2, conv_done))
    mu = jnp.mean(d, axis=-1, keepdims=True)
    dc = d - mu
    var = jnp.mean(dc * dc, axis=-1, keepdims=True)
    dn = dc * lax.rsqrt(var + EPS) * g_ln_ref[...] + (b_ln_ref[...] + _widen(side_done, D_CONV))
    dn = dn * _sigmoid(dn)
    ccm_out = _dot(dn.astype(_BF16), w_co_ref[...])

    merged = (jnp.concatenate(s_lru_blocks, axis=-1) * lru_out
              + jnp.concatenate(s_ccm_blocks, axis=-1) * ccm_out)
    run_ffn(len(ffn_work), _zero_after(ccm_out))
    x1 = x + (_dot(merged.astype(_BF16), w_out_ref[...]) + _widen(ffn_done, D_MODEL))
    x1_buf[...] = x1

    y_tm = _rms_norm(jnp.concatenate(x2_blocks, axis=-1), g_final_ref[...])
    for j in range(LANE_BLOCKS):
        perm_buf[j] = y_tm[:, _lane_block(j)]

    def write_y(y_ref):
        for b in range(nb):
            y_ref[b] = jnp.concatenate(
                [perm_buf[j, pl.ds(b, tt, stride=nb), :] for j in range(LANE_BLOCKS)], axis=-1)

    pl.when((step >= 1) & (step <= tiling.n_prompt))(functools.partial(write_y, yp_ref))
    pl.when(step > tiling.n_prompt)(functools.partial(write_y, ys_ref))

    lru_tail = lru_buf[rows:rows + lru_hist, :]
    lru_buf[0:lru_hist, :] = lru_tail
    ccm_tails = []
    for j in range(LANE_BLOCKS):
        ccm_tail = ccm_buf[j, rows:rows + ccm_hist, :]
        ccm_buf[j, 0:ccm_hist, :] = ccm_tail
        ccm_tails.append(ccm_tail)

    @pl.when(tiling.is_group_end(step))
    def _write_final_state():
        lb_out_ref[...] = lru_tail
        cb_out_ref[...] = jnp.concatenate(ccm_tails, axis=-1)
        h_out_ref[...] = h_state[...]


def _resident(shape):
    return pl.BlockSpec(shape, lambda s: (0,) * len(shape), pipeline_mode=pl.Buffered(1))


def _nbytes(shape, dtype):
    n = jnp.dtype(dtype).itemsize
    for s in shape:
        n *= s
    return n


def _layer_call(x_prompt, x_sample, h0, lb0, cb0, params):
    tiling = _Tiling(x_prompt.shape[1], x_sample.shape[0], x_sample.shape[1])
    assert x_prompt.shape[0] == GROUP_STREAMS
    nb, tt = GROUP_STREAMS, tiling.tt
    rows = nb * tt
    lru_hist = (LRU_CONV - 1) * nb
    ccm_hist = (CCM_KERNEL - 1) * nb
    d_ff = params[-4].shape[1]
    n_p, n_s = tiling.n_prompt, tiling.n_sample
    tile = (nb, tt, D_MODEL)
    state_shapes = [(nb, D_LRU), (lru_hist, D_LRU), (ccm_hist, D_CONV)]

    def sample_block(q):
        i = tiling.sample_index(q)
        return (i // n_s, i % n_s, 0)

    x_specs = [pl.BlockSpec(tile, lambda q: (0, jnp.clip(q, 0, n_p - 1), 0)),
               pl.BlockSpec(tile, sample_block)]
    y_specs = [pl.BlockSpec(tile, lambda q: (0, jnp.clip(q - 1, 0, n_p - 1), 0)),
               pl.BlockSpec(tile, lambda q: sample_block(q - 1))]
    state_specs = [pl.BlockSpec((None,) + s, lambda q: (tiling.group(q), 0, 0))
                   for s in state_shapes]
    row_tile_bytes = _nbytes((rows, D_MODEL), _F32)
    scratch = [
        ((lru_hist + rows, D_LRU), _F32),
        ((LANE_BLOCKS, ccm_hist + rows, V7X_LANES), _F32),
        ((nb, D_LRU), _F32),
        ((LANE_BLOCKS, rows, V7X_LANES), _F32),
        ((rows, D_MODEL), _F32),
    ]
    need = (sum(_nbytes(a.shape, a.dtype) for a in params)
            + 4 * sum(_nbytes(s, _F32) for s in state_shapes)
            + 8 * row_tile_bytes
            + sum(_nbytes(s, d) for s, d in scratch)
            + 12 * row_tile_bytes + 2 * _nbytes((rows, d_ff), _F32))
    assert need <= V7X_VMEM_BYTES, need
    n_groups = tiling.n_groups
    return pl.pallas_call(
        functools.partial(_layer_kernel, tiling=tiling),
        grid=(tiling.n_tiles + 1,),
        in_specs=x_specs + state_specs + [_resident(a.shape) for a in params],
        out_specs=y_specs + state_specs,
        out_shape=[jax.ShapeDtypeStruct(x_prompt.shape, _F32),
                   jax.ShapeDtypeStruct(x_sample.shape, _F32)]
        + [jax.ShapeDtypeStruct((n_groups,) + s, _F32) for s in state_shapes],
        scratch_shapes=[pltpu.VMEM(s, d) for s, d in scratch],
        compiler_params=pltpu.CompilerParams(
            dimension_semantics=("arbitrary",), vmem_limit_bytes=int(need)),
        name="layer",
    )(x_prompt, x_sample, h0, lb0, cb0, *params)


def _pack_kernel(w_ref, o_ref):
    o_ref[...] = pltpu.bitcast(w_ref[...].astype(_BF16), jnp.uint32)


def _pack_rows(w):
    k, n = w.shape
    rows = min(PACK_ROWS, k)
    return pl.pallas_call(
        _pack_kernel,
        grid=(k // rows,),
        in_specs=[pl.BlockSpec((rows, n), lambda s: (s, 0))],
        out_specs=pl.BlockSpec((rows // 2, n), lambda s: (s, 0)),
        out_shape=jax.ShapeDtypeStruct((k // 2, n), jnp.uint32),
        name="pack_rows",
    )(w)


def _pack_gate_heads(w):
    per = GATE_BLOCK // LRU_HEAD_DIM
    w = w.reshape(LRU_HEADS // per, per, LRU_HEAD_DIM, LRU_HEAD_DIM)
    eye = jnp.eye(per, dtype=w.dtype)
    blocks = jnp.einsum("gpde,pq->gpdqe", w, eye)
    packed = _pack_rows(blocks.reshape(D_LRU, GATE_BLOCK))
    return packed.reshape(D_LRU // GATE_BLOCK, GATE_BLOCK // 2, GATE_BLOCK)


def _group_time_major(a):
    streams, frames, c = a.shape
    a = a.reshape(streams // GROUP_STREAMS, GROUP_STREAMS, frames, c)
    return jnp.swapaxes(a, 1, 2).reshape(streams // GROUP_STREAMS, frames * GROUP_STREAMS, c)


def _group_stream_major(a):
    groups, rows, c = a.shape
    a = a.reshape(groups, rows // GROUP_STREAMS, GROUP_STREAMS, c)
    return jnp.swapaxes(a, 1, 2).reshape(groups * GROUP_STREAMS, rows // GROUP_STREAMS, c)


def kernel(x_prompt, x_sample, state_lru_h, cache_lru_conv, cache_ccm_conv, g_mix, w_in, b_in, w_lru_conv, b_lru_conv, w_rg_r, b_rg_r, w_rg_i, b_rg_i, lru_lambda, w_lru_o, w_ccm_dw, b_ccm_dw, g_ccm_ln, b_ccm_ln, w_ccm_o, w_out, g_ffn, w_ffn_gate, w_ffn_up, w_ffn_down, g_final):
    assert g_mix.shape[0] == 1, "single-layer trunk"
    row = lambda v: v.reshape(1, -1)
    params = (
        row(g_mix[0]), _pack_rows(w_in[0]), row(b_in[0]), w_lru_conv[0], row(b_lru_conv[0]),
        _pack_gate_heads(w_rg_r[0]), row(b_rg_r[0]), _pack_gate_heads(w_rg_i[0]), row(b_rg_i[0]),
        row(lru_lambda[0]), _pack_rows(w_lru_o[0]),
        w_ccm_dw[0], row(b_ccm_dw[0]), row(g_ccm_ln[0]), row(b_ccm_ln[0]),
        _pack_rows(w_ccm_o[0]), _pack_rows(w_out[0]),
        row(g_ffn[0]), _pack_rows(w_ffn_gate[0]), _pack_rows(w_ffn_up[0]),
        _pack_rows(w_ffn_down[0]), row(g_final))

    bp = x_prompt.shape[0]
    dt = x_prompt.dtype

    def with_prompt(sample_state, frames):
        zeros = jnp.zeros((bp, frames, sample_state.shape[-1]), dt)
        return _group_time_major(jnp.concatenate([zeros, sample_state], axis=0))

    h0 = with_prompt(state_lru_h[0][:, None, :], 1)
    lb0 = with_prompt(cache_lru_conv[0], LRU_CONV - 1)
    cb0 = with_prompt(cache_ccm_conv[0], CCM_KERNEL - 1)

    y_p, y_s, h_last, lru_tail, ccm_tail = _layer_call(x_prompt, x_sample, h0, lb0, cb0, params)
    h_last = _group_stream_major(h_last)[:, 0, :]
    lru_tail = _group_stream_major(lru_tail)
    ccm_tail = _group_stream_major(ccm_tail)
    return (y_p, y_s, h_last[None, :bp], lru_tail[None, :bp], ccm_tail[None, :bp],
            h_last[None, bp:], lru_tail[None, bp:], ccm_tail[None, bp:])
```

```python
import functools

import jax
import jax.numpy as jnp
from jax import lax
from jax.experimental import pallas as pl
from jax.experimental.pallas import tpu as pltpu

D_MODEL = 1024
D_LRU = 1024
D_CONV = 1024
LRU_HEADS = 16
LRU_HEAD_DIM = D_LRU // LRU_HEADS
LRU_CONV = 4
CCM_KERNEL = 31
RG_C = 8.0
EPS = 1e-6

V7X_MXU_DIM = 256
V7X_LANES = 128
V7X_SUBLANES = 8
V7X_VMEM_BYTES = 64 * 1024 * 1024
GATE_BLOCK = V7X_MXU_DIM
LANE_BLOCKS = D_MODEL // V7X_LANES
MXU_BLOCKS = D_MODEL // V7X_MXU_DIM
GROUP_STREAMS = V7X_SUBLANES
ROW_TILE = 128
PACK_ROWS = 128

_F32 = jnp.float32
_BF16 = jnp.bfloat16


def _dot(a, w_packed):
    return jnp.dot(a, pltpu.bitcast(w_packed, _BF16), preferred_element_type=_F32)


def _sigmoid(x):
    return 0.5 * jnp.tanh(0.5 * x) + 0.5


def _rms_norm(x, g):
    return x * lax.rsqrt(jnp.mean(x * x, axis=-1, keepdims=True) + EPS) * g


def _lane_block(j):
    return slice(j * V7X_LANES, (j + 1) * V7X_LANES)


def _mxu_block(n):
    return slice(n * V7X_MXU_DIM, (n + 1) * V7X_MXU_DIM)


def _zero_after(value):
    bits = value[0:1, 0:V7X_LANES].astype(jnp.int32)
    zero = lax.shift_right_logical(lax.shift_right_logical(bits, 16), 16)
    return zero.astype(_F32)


def _widen(zero_row, width):
    return jnp.concatenate([zero_row] * (width // V7X_LANES), axis=-1)


class _Tiling:
    def __init__(self, prompt_seq, sample_streams, sample_seq):
        self.tt = ROW_TILE // GROUP_STREAMS
        assert prompt_seq % self.tt == 0 and sample_seq % self.tt == 0
        assert sample_streams % GROUP_STREAMS == 0
        self.n_prompt = prompt_seq // self.tt
        self.n_sample = sample_seq // self.tt
        self.sample_groups = sample_streams // GROUP_STREAMS
        self.n_tiles = self.n_prompt + self.sample_groups * self.n_sample
        self.n_groups = 1 + self.sample_groups

    def sample_index(self, q):
        return jnp.clip(q - self.n_prompt, 0, self.sample_groups * self.n_sample - 1)

    def group(self, q):
        q = jnp.clip(q, 0, self.n_tiles - 1)
        return jnp.where(q < self.n_prompt, 0, 1 + self.sample_index(q) // self.n_sample)

    def is_group_start(self, q):
        return (q == 0) | ((q >= self.n_prompt) & (q < self.n_tiles)
                           & (self.sample_index(q) % self.n_sample == 0))

    def is_group_end(self, q):
        return ((q == self.n_prompt - 1)
                | ((q >= self.n_prompt) & (q < self.n_tiles)
                   & (self.sample_index(q) % self.n_sample == self.n_sample - 1)))


def _layer_kernel(xp_ref, xs_ref, h0_ref, lb0_ref, cb0_ref,
                  g_mix_ref, w_in_ref, b_in_ref, w_lc_ref, b_lc_ref,
                  w_r_ref, b_r_ref, w_i_ref, b_i_ref, lam_ref, w_lo_ref,
                  w_dw_ref, b_dw_ref, g_ln_ref, b_ln_ref, w_co_ref, w_out_ref,
                  g_ffn_ref, w_gate_ref, w_up_ref, w_down_ref, g_final_ref,
                  yp_ref, ys_ref, h_out_ref, lb_out_ref, cb_out_ref,
                  lru_buf, ccm_buf, h_state, perm_buf, x1_buf, *, tiling):
    step = pl.program_id(0)
    nb, tt = GROUP_STREAMS, tiling.tt
    rows = nb * tt
    lru_hist = (LRU_CONV - 1) * nb
    ccm_hist = (CCM_KERNEL - 1) * nb
    d_ff = w_gate_ref.shape[1]
    ffn_blocks = d_ff // V7X_MXU_DIM

    @pl.when(step == 0)
    def _clear_ffn_input():
        x1_buf[...] = jnp.zeros_like(x1_buf)

    @pl.when(tiling.is_group_start(step))
    def _load_carried_state():
        lru_buf[0:lru_hist, :] = lb0_ref[...]
        for j in range(LANE_BLOCKS):
            ccm_buf[j, 0:ccm_hist, :] = cb0_ref[:, _lane_block(j)]
        h_state[...] = h0_ref[...]

    x1_prev = x1_buf[...]
    h2 = _rms_norm(x1_prev, g_ffn_ref[...]).astype(_BF16)
    ff_blocks, x2_blocks = [], []

    def ffn_gate_up(n, after):
        cols = _mxu_block(n)
        gate = _dot(h2, w_gate_ref[:, cols]) + _widen(after, V7X_MXU_DIM)
        up = _dot(h2, w_up_ref[:, cols])
        ff_blocks.append((gate * _sigmoid(gate) * up).astype(_BF16))
        return up

    def ffn_down(n, after):
        cols = _mxu_block(n)
        ff = jnp.concatenate(ff_blocks, axis=-1)
        down = _dot(ff, w_down_ref[:, cols]) + _widen(after, V7X_MXU_DIM)
        x2_blocks.append(x1_prev[:, cols] + down)
        return down

    ffn_work = ([functools.partial(ffn_gate_up, n) for n in range(ffn_blocks)]
                + [functools.partial(ffn_down, n) for n in range(MXU_BLOCKS)])

    def run_ffn(count, after):
        last = None
        for _ in range(count):
            last = ffn_work.pop(0)(after)
        return last

    no_wait = jnp.zeros((1, V7X_LANES), _F32)

    from_prompt = step < tiling.n_prompt
    for b in range(nb):
        xb = jnp.where(from_prompt, xp_ref[b], xs_ref[b])
        for j in range(LANE_BLOCKS):
            perm_buf[j, pl.ds(b, tt, stride=nb), :] = xb[:, _lane_block(j)]
    x = jnp.concatenate([perm_buf[j] for j in range(LANE_BLOCKS)], axis=-1)
    h = _rms_norm(x, g_mix_ref[...]).astype(_BF16)

    def proj(split):
        cols = slice(split * D_MODEL, (split + 1) * D_MODEL)
        return _dot(h, w_in_ref[:, cols]) + b_in_ref[:, cols]

    def proj_block(split, n, after):
        c0 = split * D_MODEL + n * V7X_MXU_DIM
        cols = slice(c0, c0 + V7X_MXU_DIM)
        return _dot(h, w_in_ref[:, cols]) + (b_in_ref[:, cols] + _widen(after, V7X_MXU_DIM))

    xl = proj(0)
    lru_buf[lru_hist:lru_hist + rows, :] = xl
    ffn_done = _zero_after(run_ffn(1, no_wait))
    xc = (b_lc_ref[...] + _widen(ffn_done, D_LRU)) + w_lc_ref[0:1, :] * lru_buf[0:rows, :]
    for k in range(1, LRU_CONV):
        xc = xc + w_lc_ref[k:k + 1, :] * lru_buf[k * nb:k * nb + rows, :]
    xc_b = xc.astype(_BF16)
    ffn_done = _zero_after(run_ffn(2, _zero_after(xl)))

    def block_diag_dot(w_ref):
        return jnp.concatenate(
            [_dot(xc_b[:, _mxu_block(g)], w_ref[g]) for g in range(D_LRU // GATE_BLOCK)], axis=-1)

    r = _sigmoid(block_diag_dot(w_r_ref) + b_r_ref[...])
    i = _sigmoid(block_diag_dot(w_i_ref) + (b_i_ref[...] + _widen(ffn_done, D_LRU)))
    neg_lam = -lam_ref[...]
    softplus = jnp.maximum(neg_lam, 0.0) + jnp.log1p(jnp.exp(-jnp.abs(neg_lam)))
    a = jnp.exp((-RG_C * softplus) * r)
    u = jnp.sqrt(1.0 - a * a) * (i * xc)
    ffn_done = _zero_after(run_ffn(1, _zero_after(xc)))

    carry = h_state[...]
    frames = []
    for t in range(tt):
        carry = a[t * nb:(t + 1) * nb, :] * carry + u[t * nb:(t + 1) * nb, :]
        frames.append(carry)
    h_state[...] = carry
    hs = jnp.concatenate(frames, axis=0)

    glu = ((_dot(h, w_in_ref[:, 2 * D_MODEL:3 * D_MODEL])
            + (b_in_ref[:, 2 * D_MODEL:3 * D_MODEL] + _widen(ffn_done, D_MODEL)))
           * _sigmoid(proj(3)))
    for j in range(LANE_BLOCKS):
        ccm_buf[j, ccm_hist:ccm_hist + rows, :] = glu[:, _lane_block(j)]

    gated_blocks, lru_out_blocks, s_lru_blocks, s_ccm_blocks = [], [], [], []

    def gate_branch_a(n, after):
        gl = proj_block(1, n, after)
        gated_blocks.append((jax.nn.gelu(gl) * hs[:, _mxu_block(n)]).astype(_BF16))
        return gl

    def project_branch_a(n, after):
        gated = jnp.concatenate(gated_blocks, axis=-1)
        lru_out_blocks.append(
            _dot(gated, w_lo_ref[:, _mxu_block(n)]) + _widen(after, V7X_MXU_DIM))
        return lru_out_blocks[-1]

    def merge_gate_a(n, after):
        s_lru_blocks.append(_sigmoid(proj_block(4, n, after)))
        return s_lru_blocks[-1]

    def merge_gate_b(n, after):
        s_ccm_blocks.append(_sigmoid(proj_block(5, n, after)))
        return s_ccm_blocks[-1]

    side_work = [functools.partial(f, n)
                 for f in (gate_branch_a, project_branch_a, merge_gate_a, merge_gate_b)
                 for n in range(MXU_BLOCKS)]
    side_per_block = len(side_work) // LANE_BLOCKS
    ffn_during_conv = len(ffn_work) - MXU_BLOCKS
    ffn_per_block = -(-ffn_during_conv // LANE_BLOCKS)

    d_blocks = []
    conv_done = _zero_after(glu)
    side_done = conv_done
    for j in range(LANE_BLOCKS):
        cols = _lane_block(j)
        dj = (b_dw_ref[:, cols] + side_done) + w_dw_ref[0:1, cols] * ccm_buf[j, 0:rows, :]
        for k in range(1, CCM_KERNEL):
            dj = dj + w_dw_ref[k:k + 1, cols] * ccm_buf[j, k * nb:k * nb + rows, :]
        d_blocks.append(dj)
        for work in side_work[j * side_per_block:(j + 1) * side_per_block]:
            last = work(conv_done)
        n_ffn = min(ffn_per_block, len(ffn_work) - MXU_BLOCKS)
        if n_ffn:
            last = run_ffn(n_ffn, conv_done)
        conv_done = _zero_after(
            jnp.max(dj.reshape(rows // V7X_SUBLANES, V7X_SUBLANES, V7X_LANES), axis=0))
        side_done = _zero_after(last[rows - V7X_SUBLANES:rows, :])
    d = jnp.concatenate(d_blocks, axis=-1)
    lru_out = jnp.concatenate(lru_out_blocks, axis=-1)

    ffn_done = _zero_after(run_ffn(MXU_BLOCKS // 2, conv_done))
    mu = jnp.mean(d, axis=-1, keepdims=True)
    dc = d - mu
    var = jnp.mean(dc * dc, axis=-1, keepdims=True)
    dn = dc * lax.rsqrt(var + EPS) * g_ln_ref[...] + (b_ln_ref[...] + _widen(side_done, D_CONV))
    dn = dn * _sigmoid(dn)
    ccm_out = _dot(dn.astype(_BF16), w_co_ref[...])

    merged = (jnp.concatenate(s_lru_blocks, axis=-1) * lru_out
              + jnp.concatenate(s_ccm_blocks, axis=-1) * ccm_out)
    run_ffn(len(ffn_work), _zero_after(ccm_out))
    x1 = x + (_dot(merged.astype(_BF16), w_out_ref[...]) + _widen(ffn_done, D_MODEL))
    x1_buf[...] = x1

    y_tm = _rms_norm(jnp.concatenate(x2_blocks, axis=-1), g_final_ref[...])
    for j in range(LANE_BLOCKS):
        perm_buf[j] = y_tm[:, _lane_block(j)]

    def write_y(y_ref):
        for b in range(nb):
            y_ref[b] = jnp.concatenate(
                [perm_buf[j, pl.ds(b, tt, stride=nb), :] for j in range(LANE_BLOCKS)], axis=-1)

    pl.when((step >= 1) & (step <= tiling.n_prompt))(functools.partial(write_y, yp_ref))
    pl.when(step > tiling.n_prompt)(functools.partial(write_y, ys_ref))

    lru_tail = lru_buf[rows:rows + lru_hist, :]
    lru_buf[0:lru_hist, :] = lru_tail
    ccm_tails = []
    for j in range(LANE_BLOCKS):
        ccm_tail = ccm_buf[j, rows:rows + ccm_hist, :]
        ccm_buf[j, 0:ccm_hist, :] = ccm_tail
        ccm_tails.append(ccm_tail)

    @pl.when(tiling.is_group_end(step))
    def _write_final_state():
        lb_out_ref[...] = lru_tail
        cb_out_ref[...] = jnp.concatenate(ccm_tails, axis=-1)
        h_out_ref[...] = h_state[...]


def _resident(shape):
    return pl.BlockSpec(shape, lambda s: (0,) * len(shape), pipeline_mode=pl.Buffered(1))


def _nbytes(shape, dtype):
    n = jnp.dtype(dtype).itemsize
    for s in shape:
        n *= s
    return n


def _layer_call(x_prompt, x_sample, h0, lb0, cb0, params):
    tiling = _Tiling(x_prompt.shape[1], x_sample.shape[0], x_sample.shape[1])
    assert x_prompt.shape[0] == GROUP_STREAMS
    nb, tt = GROUP_STREAMS, tiling.tt
    rows = nb * tt
    lru_hist = (LRU_CONV - 1) * nb
    ccm_hist = (CCM_KERNEL - 1) * nb
    d_ff = params[-4].shape[1]
    n_p, n_s = tiling.n_prompt, tiling.n_sample
    tile = (nb, tt, D_MODEL)
    state_shapes = [(nb, D_LRU), (lru_hist, D_LRU), (ccm_hist, D_CONV)]

    def sample_block(q):
        i = tiling.sample_index(q)
        return (i // n_s, i % n_s, 0)

    x_specs = [pl.BlockSpec(tile, lambda q: (0, jnp.clip(q, 0, n_p - 1), 0)),
               pl.BlockSpec(tile, sample_block)]
    y_specs = [pl.BlockSpec(tile, lambda q: (0, jnp.clip(q - 1, 0, n_p - 1), 0)),
               pl.BlockSpec(tile, lambda q: sample_block(q - 1))]
    state_specs = [pl.BlockSpec((None,) + s, lambda q: (tiling.group(q), 0, 0))
                   for s in state_shapes]
    row_tile_bytes = _nbytes((rows, D_MODEL), _F32)
    scratch = [
        ((lru_hist + rows, D_LRU), _F32),
        ((LANE_BLOCKS, ccm_hist + rows, V7X_LANES), _F32),
        ((nb, D_LRU), _F32),
        ((LANE_BLOCKS, rows, V7X_LANES), _F32),
        ((rows, D_MODEL), _F32),
    ]
    need = (sum(_nbytes(a.shape, a.dtype) for a in params)
            + 4 * sum(_nbytes(s, _F32) for s in state_shapes)
            + 8 * row_tile_bytes
            + sum(_nbytes(s, d) for s, d in scratch)
            + 12 * row_tile_bytes + 2 * _nbytes((rows, d_ff), _F32))
    assert need <= V7X_VMEM_BYTES, need
    n_groups = tiling.n_groups
    return pl.pallas_call(
        functools.partial(_layer_kernel, tiling=tiling),
        grid=(tiling.n_tiles + 1,),
        in_specs=x_specs + state_specs + [_resident(a.shape) for a in params],
        out_specs=y_specs + state_specs,
        out_shape=[jax.ShapeDtypeStruct(x_prompt.shape, _F32),
                   jax.ShapeDtypeStruct(x_sample.shape, _F32)]
        + [jax.ShapeDtypeStruct((n_groups,) + s, _F32) for s in state_shapes],
        scratch_shapes=[pltpu.VMEM(s, d) for s, d in scratch],
        compiler_params=pltpu.CompilerParams(
            dimension_semantics=("arbitrary",), vmem_limit_bytes=int(need)),
        name="layer",
    )(x_prompt, x_sample, h0, lb0, cb0, *params)


def _pack_kernel(*refs, row_blocks):
    step = pl.program_id(0)
    w_refs, o_refs = refs[:len(row_blocks)], refs[len(row_blocks):]
    for w_ref, o_ref, n_blocks in zip(w_refs, o_refs, row_blocks):
        @pl.when(step < n_blocks)
        def _pack(w_ref=w_ref, o_ref=o_ref):
            o_ref[...] = pltpu.bitcast(w_ref[...].astype(_BF16), jnp.uint32)


def _pack_rows(weights):
    row_blocks = tuple(w.shape[0] // PACK_ROWS for w in weights)
    assert all(w.shape[0] % PACK_ROWS == 0 for w in weights)

    def spec(w, rows, last):
        return pl.BlockSpec((rows, w.shape[1]), lambda s: (jnp.minimum(s, last), 0))

    block_bytes = sum(_nbytes((PACK_ROWS, w.shape[1]), _F32) for w in weights)
    return pl.pallas_call(
        functools.partial(_pack_kernel, row_blocks=row_blocks),
        grid=(max(row_blocks),),
        in_specs=[spec(w, PACK_ROWS, n - 1) for w, n in zip(weights, row_blocks)],
        out_specs=[spec(w, PACK_ROWS // 2, n - 1) for w, n in zip(weights, row_blocks)],
        out_shape=[jax.ShapeDtypeStruct((w.shape[0] // 2, w.shape[1]), jnp.uint32)
                   for w in weights],
        compiler_params=pltpu.CompilerParams(
            dimension_semantics=("arbitrary",),
            vmem_limit_bytes=4 * block_bytes),
        name="pack_rows",
    )(*weights)


def _gate_head_blocks(w):
    per = GATE_BLOCK // LRU_HEAD_DIM
    w = w.reshape(LRU_HEADS // per, per, LRU_HEAD_DIM, LRU_HEAD_DIM)
    eye = jnp.eye(per, dtype=w.dtype)
    return jnp.einsum("gpde,pq->gpdqe", w, eye).reshape(D_LRU, GATE_BLOCK)


def _group_time_major(a):
    streams, frames, c = a.shape
    a = a.reshape(streams // GROUP_STREAMS, GROUP_STREAMS, frames, c)
    return jnp.swapaxes(a, 1, 2).reshape(streams // GROUP_STREAMS, frames * GROUP_STREAMS, c)


def _group_stream_major(a):
    groups, rows, c = a.shape
    a = a.reshape(groups, rows // GROUP_STREAMS, GROUP_STREAMS, c)
    return jnp.swapaxes(a, 1, 2).reshape(groups * GROUP_STREAMS, rows // GROUP_STREAMS, c)


def kernel(x_prompt, x_sample, state_lru_h, cache_lru_conv, cache_ccm_conv, g_mix, w_in, b_in, w_lru_conv, b_lru_conv, w_rg_r, b_rg_r, w_rg_i, b_rg_i, lru_lambda, w_lru_o, w_ccm_dw, b_ccm_dw, g_ccm_ln, b_ccm_ln, w_ccm_o, w_out, g_ffn, w_ffn_gate, w_ffn_up, w_ffn_down, g_final):
    assert g_mix.shape[0] == 1, "single-layer trunk"
    row = lambda v: v.reshape(1, -1)
    (p_in, p_r, p_i, p_lo, p_co, p_out, p_gate, p_up, p_down) = _pack_rows([
        w_in[0], _gate_head_blocks(w_rg_r[0]), _gate_head_blocks(w_rg_i[0]), w_lru_o[0],
        w_ccm_o[0], w_out[0], w_ffn_gate[0], w_ffn_up[0], w_ffn_down[0]])
    gate_blocks = (D_LRU // GATE_BLOCK, GATE_BLOCK // 2, GATE_BLOCK)
    params = (
        row(g_mix[0]), p_in, row(b_in[0]), w_lru_conv[0], row(b_lru_conv[0]),
        p_r.reshape(gate_blocks), row(b_rg_r[0]), p_i.reshape(gate_blocks), row(b_rg_i[0]),
        row(lru_lambda[0]), p_lo,
        w_ccm_dw[0], row(b_ccm_dw[0]), row(g_ccm_ln[0]), row(b_ccm_ln[0]),
        p_co, p_out,
        row(g_ffn[0]), p_gate, p_up, p_down, row(g_final))

    bp = x_prompt.shape[0]
    dt = x_prompt.dtype

    def with_prompt(sample_state, frames):
        zeros = jnp.zeros((bp, frames, sample_state.shape[-1]), dt)
        return _group_time_major(jnp.concatenate([zeros, sample_state], axis=0))

    h0 = with_prompt(state_lru_h[0][:, None, :], 1)
    lb0 = with_prompt(cache_lru_conv[0], LRU_CONV - 1)
    cb0 = with_prompt(cache_ccm_conv[0], CCM_KERNEL - 1)

    y_p, y_s, h_last, lru_tail, ccm_tail = _layer_call(x_prompt, x_sample, h0, lb0, cb0, params)
    h_last = _group_stream_major(h_last)[:, 0, :]
    lru_tail = _group_stream_major(lru_tail)
    ccm_tail = _group_stream_major(ccm_tail)
    return (y_p, y_s, h_last[None, :bp], lru_tail[None, :bp], ccm_tail[None, :bp],
            h_last[None, bp:], lru_tail[None, bp:], ccm_tail[None, bp:])
```

```python
import functools

import jax
import jax.numpy as jnp
from jax import lax
from jax.experimental import pallas as pl
from jax.experimental.pallas import tpu as pltpu

D_MODEL = 1024
D_LRU = 1024
D_CONV = 1024
LRU_HEADS = 16
LRU_HEAD_DIM = D_LRU // LRU_HEADS
LRU_CONV = 4
CCM_KERNEL = 31
RG_C = 8.0
EPS = 1e-6

V7X_MXU_DIM = 256
V7X_LANES = 128
V7X_SUBLANES = 8
V7X_VMEM_BYTES = 64 * 1024 * 1024
GATE_BLOCK = V7X_MXU_DIM
LANE_BLOCKS = D_MODEL // V7X_LANES
MXU_BLOCKS = D_MODEL // V7X_MXU_DIM
GROUP_STREAMS = V7X_SUBLANES
ROW_TILE = 128
PACK_ROWS = 128

_F32 = jnp.float32
_BF16 = jnp.bfloat16


def _dot(a, w_packed):
    return jnp.dot(a, pltpu.bitcast(w_packed, _BF16), preferred_element_type=_F32)


def _sigmoid(x):
    return 0.5 * jnp.tanh(0.5 * x) + 0.5


def _rms_norm(x, g):
    return x * lax.rsqrt(jnp.mean(x * x, axis=-1, keepdims=True) + EPS) * g


def _lane_block(j):
    return slice(j * V7X_LANES, (j + 1) * V7X_LANES)


def _mxu_block(n):
    return slice(n * V7X_MXU_DIM, (n + 1) * V7X_MXU_DIM)


def _zero_after(value):
    bits = value[0:1, 0:V7X_LANES].astype(jnp.int32)
    zero = lax.shift_right_logical(lax.shift_right_logical(bits, 16), 16)
    return zero.astype(_F32)


def _widen(zero_row, width):
    return jnp.concatenate([zero_row] * (width // V7X_LANES), axis=-1)


class _Tiling:
    def __init__(self, prompt_seq, sample_streams, sample_seq):
        self.tt = ROW_TILE // GROUP_STREAMS
        assert prompt_seq % self.tt == 0 and sample_seq % self.tt == 0
        assert sample_streams % GROUP_STREAMS == 0
        self.n_prompt = prompt_seq // self.tt
        self.n_sample = sample_seq // self.tt
        self.sample_groups = sample_streams // GROUP_STREAMS
        self.n_tiles = self.n_prompt + self.sample_groups * self.n_sample
        self.n_groups = 1 + self.sample_groups

    def sample_index(self, q):
        return jnp.clip(q - self.n_prompt, 0, self.sample_groups * self.n_sample - 1)

    def group(self, q):
        q = jnp.clip(q, 0, self.n_tiles - 1)
        return jnp.where(q < self.n_prompt, 0, 1 + self.sample_index(q) // self.n_sample)

    def is_group_start(self, q):
        return (q == 0) | ((q >= self.n_prompt) & (q < self.n_tiles)
                           & (self.sample_index(q) % self.n_sample == 0))

    def is_group_end(self, q):
        return ((q == self.n_prompt - 1)
                | ((q >= self.n_prompt) & (q < self.n_tiles)
                   & (self.sample_index(q) % self.n_sample == self.n_sample - 1)))


def _layer_kernel(xp_ref, xs_ref, h0_ref, lb0_ref, cb0_ref,
                  g_mix_ref, w_in_ref, b_in_ref, w_lc_ref, b_lc_ref,
                  w_r_ref, b_r_ref, w_i_ref, b_i_ref, lam_ref, w_lo_ref,
                  w_dw_ref, b_dw_ref, g_ln_ref, b_ln_ref, w_co_ref, w_out_ref,
                  g_ffn_ref, w_gate_ref, w_up_ref, w_down_ref, g_final_ref,
                  yp_ref, ys_ref, h_out_ref, lb_out_ref, cb_out_ref,
                  lru_buf, ccm_buf, h_state, perm_buf, x1_buf, *, tiling):
    step = pl.program_id(0)
    nb, tt = GROUP_STREAMS, tiling.tt
    rows = nb * tt
    lru_hist = (LRU_CONV - 1) * nb
    ccm_hist = (CCM_KERNEL - 1) * nb
    d_ff = w_gate_ref.shape[1]
    ffn_blocks = d_ff // V7X_MXU_DIM

    @pl.when(step == 0)
    def _clear_ffn_input():
        x1_buf[...] = jnp.zeros_like(x1_buf)

    @pl.when(tiling.is_group_start(step))
    def _load_carried_state():
        lru_buf[0:lru_hist, :] = lb0_ref[...]
        for j in range(LANE_BLOCKS):
            ccm_buf[j, 0:ccm_hist, :] = cb0_ref[:, _lane_block(j)]
        h_state[...] = h0_ref[...]

    x1_prev = x1_buf[...]
    h2 = _rms_norm(x1_prev, g_ffn_ref[...]).astype(_BF16)
    ff_blocks, x2_blocks = [], []

    def ffn_gate_up(n, after):
        cols = _mxu_block(n)
        gate = _dot(h2, w_gate_ref[:, cols]) + _widen(after, V7X_MXU_DIM)
        up = _dot(h2, w_up_ref[:, cols])
        ff_blocks.append((gate * _sigmoid(gate) * up).astype(_BF16))
        return up

    def ffn_down(n, after):
        cols = _mxu_block(n)
        ff = jnp.concatenate(ff_blocks, axis=-1)
        down = _dot(ff, w_down_ref[:, cols]) + _widen(after, V7X_MXU_DIM)
        x2_blocks.append(x1_prev[:, cols] + down)
        return down

    ffn_work = ([functools.partial(ffn_gate_up, n) for n in range(ffn_blocks)]
                + [functools.partial(ffn_down, n) for n in range(MXU_BLOCKS)])

    def run_ffn(count, after):
        last = None
        for _ in range(count):
            last = ffn_work.pop(0)(after)
        return last

    no_wait = jnp.zeros((1, V7X_LANES), _F32)

    from_prompt = step < tiling.n_prompt
    for b in range(nb):
        xb = jnp.where(from_prompt, xp_ref[b], xs_ref[b])
        for j in range(LANE_BLOCKS):
            perm_buf[j, pl.ds(b, tt, stride=nb), :] = xb[:, _lane_block(j)]
    x = jnp.concatenate([perm_buf[j] for j in range(LANE_BLOCKS)], axis=-1)
    h = _rms_norm(x, g_mix_ref[...]).astype(_BF16)

    def proj(split):
        cols = slice(split * D_MODEL, (split + 1) * D_MODEL)
        return _dot(h, w_in_ref[:, cols]) + b_in_ref[:, cols]

    def proj_block(split, n, after):
        c0 = split * D_MODEL + n * V7X_MXU_DIM
        cols = slice(c0, c0 + V7X_MXU_DIM)
        return _dot(h, w_in_ref[:, cols]) + (b_in_ref[:, cols] + _widen(after, V7X_MXU_DIM))

    xl = proj(0)
    lru_buf[lru_hist:lru_hist + rows, :] = xl
    ffn_done = _zero_after(run_ffn(1, no_wait))
    xc = (b_lc_ref[...] + _widen(ffn_done, D_LRU)) + w_lc_ref[0:1, :] * lru_buf[0:rows, :]
    for k in range(1, LRU_CONV):
        xc = xc + w_lc_ref[k:k + 1, :] * lru_buf[k * nb:k * nb + rows, :]
    xc_b = xc.astype(_BF16)
    ffn_done = _zero_after(run_ffn(2, _zero_after(xl)))

    def block_diag_dot(w_ref):
        return jnp.concatenate(
            [_dot(xc_b[:, _mxu_block(g)], w_ref[g]) for g in range(D_LRU // GATE_BLOCK)], axis=-1)

    r = _sigmoid(block_diag_dot(w_r_ref) + b_r_ref[...])
    i = _sigmoid(block_diag_dot(w_i_ref) + (b_i_ref[...] + _widen(ffn_done, D_LRU)))
    neg_lam = -lam_ref[...]
    softplus = jnp.maximum(neg_lam, 0.0) + jnp.log1p(jnp.exp(-jnp.abs(neg_lam)))
    a = jnp.exp((-RG_C * softplus) * r)
    u = jnp.sqrt(1.0 - a * a) * (i * xc)
    ffn_done = _zero_after(run_ffn(2, _zero_after(xc)))

    carry = h_state[...]
    frames = []
    for t in range(tt):
        carry = a[t * nb:(t + 1) * nb, :] * carry + u[t * nb:(t + 1) * nb, :]
        frames.append(carry)
    h_state[...] = carry
    hs = jnp.concatenate(frames, axis=0)

    glu = ((_dot(h, w_in_ref[:, 2 * D_MODEL:3 * D_MODEL])
            + (b_in_ref[:, 2 * D_MODEL:3 * D_MODEL] + _widen(ffn_done, D_MODEL)))
           * _sigmoid(proj(3)))
    for j in range(LANE_BLOCKS):
        ccm_buf[j, ccm_hist:ccm_hist + rows, :] = glu[:, _lane_block(j)]

    gated_blocks, lru_out_blocks, s_lru_blocks, s_ccm_blocks = [], [], [], []

    def gate_branch_a(n, after):
        gl = proj_block(1, n, after)
        gated_blocks.append((jax.nn.gelu(gl) * hs[:, _mxu_block(n)]).astype(_BF16))
        return gl

    def project_branch_a(n, after):
        gated = jnp.concatenate(gated_blocks, axis=-1)
        lru_out_blocks.append(
            _dot(gated, w_lo_ref[:, _mxu_block(n)]) + _widen(after, V7X_MXU_DIM))
        return lru_out_blocks[-1]

    def merge_gate_a(n, after):
        s_lru_blocks.append(_sigmoid(proj_block(4, n, after)))
        return s_lru_blocks[-1]

    def merge_gate_b(n, after):
        s_ccm_blocks.append(_sigmoid(proj_block(5, n, after)))
        return s_ccm_blocks[-1]

    side_work = [functools.partial(f, n)
                 for f in (gate_branch_a, project_branch_a, merge_gate_a, merge_gate_b)
                 for n in range(MXU_BLOCKS)]
    side_per_block = len(side_work) // LANE_BLOCKS
    ffn_during_conv = len(ffn_work) - MXU_BLOCKS
    ffn_per_block = -(-ffn_during_conv // LANE_BLOCKS)

    d_blocks = []
    conv_done = _zero_after(glu)
    side_done = conv_done
    for j in range(LANE_BLOCKS):
        cols = _lane_block(j)
        dj = (b_dw_ref[:, cols] + side_done) + w_dw_ref[0:1, cols] * ccm_buf[j, 0:rows, :]
        for k in range(1, CCM_KERNEL):
            dj = dj + w_dw_ref[k:k + 1, cols] * ccm_buf[j, k * nb:k * nb + rows, :]
        d_blocks.append(dj)
        for work in side_work[j * side_per_block:(j + 1) * side_per_block]:
            last = work(conv_done)
        n_ffn = min(ffn_per_block, len(ffn_work) - MXU_BLOCKS)
        if n_ffn:
            last = run_ffn(n_ffn, conv_done)
        conv_done = _zero_after(
            jnp.max(dj.reshape(rows // V7X_SUBLANES, V7X_SUBLANES, V7X_LANES), axis=0))
        side_done = _zero_after(last[rows - V7X_SUBLANES:rows, :])
    d = jnp.concatenate(d_blocks, axis=-1)
    lru_out = jnp.concatenate(lru_out_blocks, axis=-1)

    ffn_done = _zero_after(run_ffn(MXU_BLOCKS // 2, conv_done))
    mu = jnp.mean(d, axis=-1, keepdims=True)
    dc = d - mu
    var = jnp.mean(dc * dc, axis=-1, keepdims=True)
    dn = dc * lax.rsqrt(var + EPS) * g_ln_ref[...] + (b_ln_ref[...] + _widen(side_done, D_CONV))
    dn = dn * _sigmoid(dn)
    ccm_out = _dot(dn.astype(_BF16), w_co_ref[...])

    merged = (jnp.concatenate(s_lru_blocks, axis=-1) * lru_out
              + jnp.concatenate(s_ccm_blocks, axis=-1) * ccm_out)
    run_ffn(len(ffn_work), _zero_after(ccm_out))
    x1 = x + (_dot(merged.astype(_BF16), w_out_ref[...]) + _widen(ffn_done, D_MODEL))
    x1_buf[...] = x1

    y_tm = _rms_norm(jnp.concatenate(x2_blocks, axis=-1), g_final_ref[...])
    for j in range(LANE_BLOCKS):
        perm_buf[j] = y_tm[:, _lane_block(j)]

    def write_y(y_ref):
        for b in range(nb):
            y_ref[b] = jnp.concatenate(
                [perm_buf[j, pl.ds(b, tt, stride=nb), :] for j in range(LANE_BLOCKS)], axis=-1)

    pl.when((step >= 1) & (step <= tiling.n_prompt))(functools.partial(write_y, yp_ref))
    pl.when(step > tiling.n_prompt)(functools.partial(write_y, ys_ref))

    lru_tail = lru_buf[rows:rows + lru_hist, :]
    lru_buf[0:lru_hist, :] = lru_tail
    ccm_tails = []
    for j in range(LANE_BLOCKS):
        ccm_tail = ccm_buf[j, rows:rows + ccm_hist, :]
        ccm_buf[j, 0:ccm_hist, :] = ccm_tail
        ccm_tails.append(ccm_tail)

    @pl.when(tiling.is_group_end(step))
    def _write_final_state():
        lb_out_ref[...] = lru_tail
        cb_out_ref[...] = jnp.concatenate(ccm_tails, axis=-1)
        h_out_ref[...] = h_state[...]


def _resident(shape):
    return pl.BlockSpec(shape, lambda s: (0,) * len(shape), pipeline_mode=pl.Buffered(1))


def _nbytes(shape, dtype):
    n = jnp.dtype(dtype).itemsize
    for s in shape:
        n *= s
    return n


def _layer_call(x_prompt, x_sample, h0, lb0, cb0, params):
    tiling = _Tiling(x_prompt.shape[1], x_sample.shape[0], x_sample.shape[1])
    assert x_prompt.shape[0] == GROUP_STREAMS
    nb, tt = GROUP_STREAMS, tiling.tt
    rows = nb * tt
    lru_hist = (LRU_CONV - 1) * nb
    ccm_hist = (CCM_KERNEL - 1) * nb
    d_ff = params[-4].shape[1]
    n_p, n_s = tiling.n_prompt, tiling.n_sample
    tile = (nb, tt, D_MODEL)
    state_shapes = [(nb, D_LRU), (lru_hist, D_LRU), (ccm_hist, D_CONV)]

    def sample_block(q):
        i = tiling.sample_index(q)
        return (i // n_s, i % n_s, 0)

    x_specs = [pl.BlockSpec(tile, lambda q: (0, jnp.clip(q, 0, n_p - 1), 0)),
               pl.BlockSpec(tile, sample_block)]
    y_specs = [pl.BlockSpec(tile, lambda q: (0, jnp.clip(q - 1, 0, n_p - 1), 0)),
               pl.BlockSpec(tile, lambda q: sample_block(q - 1))]
    state_specs = [pl.BlockSpec((None,) + s, lambda q: (tiling.group(q), 0, 0))
                   for s in state_shapes]
    row_tile_bytes = _nbytes((rows, D_MODEL), _F32)
    scratch = [
        ((lru_hist + rows, D_LRU), _F32),
        ((LANE_BLOCKS, ccm_hist + rows, V7X_LANES), _F32),
        ((nb, D_LRU), _F32),
        ((LANE_BLOCKS, rows, V7X_LANES), _F32),
        ((rows, D_MODEL), _F32),
    ]
    need = (sum(_nbytes(a.shape, a.dtype) for a in params)
            + 4 * sum(_nbytes(s, _F32) for s in state_shapes)
            + 8 * row_tile_bytes
            + sum(_nbytes(s, d) for s, d in scratch)
            + 12 * row_tile_bytes + 2 * _nbytes((rows, d_ff), _F32))
    assert need <= V7X_VMEM_BYTES, need
    n_groups = tiling.n_groups
    return pl.pallas_call(
        functools.partial(_layer_kernel, tiling=tiling),
        grid=(tiling.n_tiles + 1,),
        in_specs=x_specs + state_specs + [_resident(a.shape) for a in params],
        out_specs=y_specs + state_specs,
        out_shape=[jax.ShapeDtypeStruct(x_prompt.shape, _F32),
                   jax.ShapeDtypeStruct(x_sample.shape, _F32)]
        + [jax.ShapeDtypeStruct((n_groups,) + s, _F32) for s in state_shapes],
        scratch_shapes=[pltpu.VMEM(s, d) for s, d in scratch],
        compiler_params=pltpu.CompilerParams(
            dimension_semantics=("arbitrary",), vmem_limit_bytes=int(need)),
        name="layer",
    )(x_prompt, x_sample, h0, lb0, cb0, *params)


def _pack_kernel(*refs, row_blocks):
    step = pl.program_id(0)
    w_refs, o_refs = refs[:len(row_blocks)], refs[len(row_blocks):]
    for w_ref, o_ref, n_blocks in zip(w_refs, o_refs, row_blocks):
        @pl.when(step < n_blocks)
        def _pack(w_ref=w_ref, o_ref=o_ref):
            o_ref[...] = pltpu.bitcast(w_ref[...].astype(_BF16), jnp.uint32)


def _pack_rows(weights):
    row_blocks = tuple(w.shape[0] // PACK_ROWS for w in weights)
    assert all(w.shape[0] % PACK_ROWS == 0 for w in weights)

    def spec(w, rows, last):
        return pl.BlockSpec((rows, w.shape[1]), lambda s: (jnp.minimum(s, last), 0))

    block_bytes = sum(_nbytes((PACK_ROWS, w.shape[1]), _F32) for w in weights)
    return pl.pallas_call(
        functools.partial(_pack_kernel, row_blocks=row_blocks),
        grid=(max(row_blocks),),
        in_specs=[spec(w, PACK_ROWS, n - 1) for w, n in zip(weights, row_blocks)],
        out_specs=[spec(w, PACK_ROWS // 2, n - 1) for w, n in zip(weights, row_blocks)],
        out_shape=[jax.ShapeDtypeStruct((w.shape[0] // 2, w.shape[1]), jnp.uint32)
                   for w in weights],
        compiler_params=pltpu.CompilerParams(
            dimension_semantics=("arbitrary",),
            vmem_limit_bytes=4 * block_bytes),
        name="pack_rows",
    )(*weights)


def _gate_head_blocks(w):
    per = GATE_BLOCK // LRU_HEAD_DIM
    w = w.reshape(LRU_HEADS // per, per, LRU_HEAD_DIM, LRU_HEAD_DIM)
    eye = jnp.eye(per, dtype=w.dtype)
    return jnp.einsum("gpde,pq->gpdqe", w, eye).reshape(D_LRU, GATE_BLOCK)


def _group_time_major(a):
    streams, frames, c = a.shape
    a = a.reshape(streams // GROUP_STREAMS, GROUP_STREAMS, frames, c)
    return jnp.swapaxes(a, 1, 2).reshape(streams // GROUP_STREAMS, frames * GROUP_STREAMS, c)


def _group_stream_major(a):
    groups, rows, c = a.shape
    a = a.reshape(groups, rows // GROUP_STREAMS, GROUP_STREAMS, c)
    return jnp.swapaxes(a, 1, 2).reshape(groups * GROUP_STREAMS, rows // GROUP_STREAMS, c)


def kernel(x_prompt, x_sample, state_lru_h, cache_lru_conv, cache_ccm_conv, g_mix, w_in, b_in, w_lru_conv, b_lru_conv, w_rg_r, b_rg_r, w_rg_i, b_rg_i, lru_lambda, w_lru_o, w_ccm_dw, b_ccm_dw, g_ccm_ln, b_ccm_ln, w_ccm_o, w_out, g_ffn, w_ffn_gate, w_ffn_up, w_ffn_down, g_final):
    assert g_mix.shape[0] == 1, "single-layer trunk"
    row = lambda v: v.reshape(1, -1)
    (p_in, p_r, p_i, p_lo, p_co, p_out, p_gate, p_up, p_down) = _pack_rows([
        w_in[0], _gate_head_blocks(w_rg_r[0]), _gate_head_blocks(w_rg_i[0]), w_lru_o[0],
        w_ccm_o[0], w_out[0], w_ffn_gate[0], w_ffn_up[0], w_ffn_down[0]])
    gate_blocks = (D_LRU // GATE_BLOCK, GATE_BLOCK // 2, GATE_BLOCK)
    params = (
        row(g_mix[0]), p_in, row(b_in[0]), w_lru_conv[0], row(b_lru_conv[0]),
        p_r.reshape(gate_blocks), row(b_rg_r[0]), p_i.reshape(gate_blocks), row(b_rg_i[0]),
        row(lru_lambda[0]), p_lo,
        w_ccm_dw[0], row(b_ccm_dw[0]), row(g_ccm_ln[0]), row(b_ccm_ln[0]),
        p_co, p_out,
        row(g_ffn[0]), p_gate, p_up, p_down, row(g_final))

    bp = x_prompt.shape[0]
    dt = x_prompt.dtype

    def with_prompt(sample_state, frames):
        zeros = jnp.zeros((bp, frames, sample_state.shape[-1]), dt)
        return _group_time_major(jnp.concatenate([zeros, sample_state], axis=0))

    h0 = with_prompt(state_lru_h[0][:, None, :], 1)
    lb0 = with_prompt(cache_lru_conv[0], LRU_CONV - 1)
    cb0 = with_prompt(cache_ccm_conv[0], CCM_KERNEL - 1)

    y_p, y_s, h_last, lru_tail, ccm_tail = _layer_call(x_prompt, x_sample, h0, lb0, cb0, params)
    h_last = _group_stream_major(h_last)[:, 0, :]
    lru_tail = _group_stream_major(lru_tail)
    ccm_tail = _group_stream_major(ccm_tail)
    return (y_p, y_s, h_last[None, :bp], lru_tail[None, :bp], ccm_tail[None, :bp],
            h_last[None, bp:], lru_tail[None, bp:], ccm_tail[None, bp:])
```

```python
import functools

import jax
import jax.numpy as jnp
from jax import lax
from jax.experimental import pallas as pl
from jax.experimental.pallas import tpu as pltpu

D_MODEL = 1024
D_LRU = 1024
D_CONV = 1024
LRU_HEADS = 16
LRU_HEAD_DIM = D_LRU // LRU_HEADS
LRU_CONV = 4
CCM_KERNEL = 31
RG_C = 8.0
EPS = 1e-6

V7X_MXU_DIM = 256
V7X_LANES = 128
V7X_SUBLANES = 8
V7X_VMEM_BYTES = 64 * 1024 * 1024
GATE_BLOCK = V7X_MXU_DIM
LANE_BLOCKS = D_MODEL // V7X_LANES
MXU_BLOCKS = D_MODEL // V7X_MXU_DIM
GROUP_STREAMS = V7X_SUBLANES
ROW_TILE = 128
PACK_ROWS = 128
STATE_FRAMES = 1 + (LRU_CONV - 1) + (CCM_KERNEL - 1)

_F32 = jnp.float32
_BF16 = jnp.bfloat16


def _dot(a, w_packed):
    return jnp.dot(a, pltpu.bitcast(w_packed, _BF16), preferred_element_type=_F32)


def _sigmoid(x):
    return 0.5 * jnp.tanh(0.5 * x) + 0.5


def _rms_norm(x, g):
    return x * lax.rsqrt(jnp.mean(x * x, axis=-1, keepdims=True) + EPS) * g


def _lane_block(j):
    return slice(j * V7X_LANES, (j + 1) * V7X_LANES)


def _mxu_block(n):
    return slice(n * V7X_MXU_DIM, (n + 1) * V7X_MXU_DIM)


def _zero_after(value):
    bits = value[0:1, 0:V7X_LANES].astype(jnp.int32)
    zero = lax.shift_right_logical(lax.shift_right_logical(bits, 16), 16)
    return zero.astype(_F32)


def _widen(zero_row, width):
    return jnp.concatenate([zero_row] * (width // V7X_LANES), axis=-1)


class _Tiling:
    def __init__(self, prompt_seq, sample_streams, sample_seq):
        self.tt = ROW_TILE // GROUP_STREAMS
        assert prompt_seq % self.tt == 0 and sample_seq % self.tt == 0
        assert sample_streams % GROUP_STREAMS == 0
        self.n_prompt = prompt_seq // self.tt
        self.n_sample = sample_seq // self.tt
        self.sample_groups = sample_streams // GROUP_STREAMS
        self.n_tiles = self.n_prompt + self.sample_groups * self.n_sample
        self.n_groups = 1 + self.sample_groups

    def sample_index(self, q):
        return jnp.clip(q - self.n_prompt, 0, self.sample_groups * self.n_sample - 1)

    def group(self, q):
        q = jnp.clip(q, 0, self.n_tiles - 1)
        return jnp.where(q < self.n_prompt, 0, 1 + self.sample_index(q) // self.n_sample)

    def is_group_start(self, q):
        return (q == 0) | ((q >= self.n_prompt) & (q < self.n_tiles)
                           & (self.sample_index(q) % self.n_sample == 0))

    def is_group_end(self, q):
        return ((q == self.n_prompt - 1)
                | ((q >= self.n_prompt) & (q < self.n_tiles)
                   & (self.sample_index(q) % self.n_sample == self.n_sample - 1)))


def _layer_kernel(xp_ref, xs_ref, state_ref,
                  g_mix_ref, w_in_ref, b_in_ref, w_lc_ref, b_lc_ref,
                  w_r_ref, b_r_ref, w_i_ref, b_i_ref, lam_ref, w_lo_ref,
                  w_dw_ref, b_dw_ref, g_ln_ref, b_ln_ref, w_co_ref, w_out_ref,
                  g_ffn_ref, w_gate_ref, w_up_ref, w_down_ref, g_final_ref,
                  yp_ref, ys_ref, state_out_ref,
                  lru_buf, ccm_buf, h_state, perm_buf, x1_buf, *, tiling):
    step = pl.program_id(0)
    nb, tt = GROUP_STREAMS, tiling.tt
    rows = nb * tt
    lru_hist = (LRU_CONV - 1) * nb
    ccm_hist = (CCM_KERNEL - 1) * nb
    state_rows = STATE_FRAMES * nb
    d_ff = w_gate_ref.shape[1]
    ffn_blocks = d_ff // V7X_MXU_DIM

    @pl.when(step == 0)
    def _clear_ffn_input():
        x1_buf[...] = jnp.zeros_like(x1_buf)

    @pl.when(tiling.is_group_start(step))
    def _load_carried_state():
        h_state[...] = state_ref[0:nb, :]
        lru_buf[0:lru_hist, :] = state_ref[nb:nb + lru_hist, :]
        for j in range(LANE_BLOCKS):
            ccm_buf[j, 0:ccm_hist, :] = state_ref[nb + lru_hist:state_rows, _lane_block(j)]

    x1_prev = x1_buf[...]
    h2 = _rms_norm(x1_prev, g_ffn_ref[...]).astype(_BF16)
    ff_blocks, x2_blocks = [], []

    def ffn_gate_up(n, after):
        cols = _mxu_block(n)
        gate = _dot(h2, w_gate_ref[:, cols]) + _widen(after, V7X_MXU_DIM)
        up = _dot(h2, w_up_ref[:, cols])
        ff_blocks.append((gate * _sigmoid(gate) * up).astype(_BF16))
        return up

    def ffn_down(n, after):
        cols = _mxu_block(n)
        ff = jnp.concatenate(ff_blocks, axis=-1)
        down = _dot(ff, w_down_ref[:, cols]) + _widen(after, V7X_MXU_DIM)
        x2_blocks.append(x1_prev[:, cols] + down)
        return down

    ffn_work = ([functools.partial(ffn_gate_up, n) for n in range(ffn_blocks)]
                + [functools.partial(ffn_down, n) for n in range(MXU_BLOCKS)])

    def run_ffn(count, after):
        last = None
        for _ in range(count):
            last = ffn_work.pop(0)(after)
        return last

    no_wait = jnp.zeros((1, V7X_LANES), _F32)

    from_prompt = step < tiling.n_prompt
    for b in range(nb):
        xb = jnp.where(from_prompt, xp_ref[b], xs_ref[b])
        for j in range(LANE_BLOCKS):
            perm_buf[j, pl.ds(b, tt, stride=nb), :] = xb[:, _lane_block(j)]
    x = jnp.concatenate([perm_buf[j] for j in range(LANE_BLOCKS)], axis=-1)
    h = _rms_norm(x, g_mix_ref[...]).astype(_BF16)

    def proj(split):
        cols = slice(split * D_MODEL, (split + 1) * D_MODEL)
        return _dot(h, w_in_ref[:, cols]) + b_in_ref[:, cols]

    def proj_block(split, n, after):
        c0 = split * D_MODEL + n * V7X_MXU_DIM
        cols = slice(c0, c0 + V7X_MXU_DIM)
        return _dot(h, w_in_ref[:, cols]) + (b_in_ref[:, cols] + _widen(after, V7X_MXU_DIM))

    xl = proj(0)
    lru_buf[lru_hist:lru_hist + rows, :] = xl
    ffn_done = _zero_after(run_ffn(1, no_wait))
    xc = (b_lc_ref[...] + _widen(ffn_done, D_LRU)) + w_lc_ref[0:1, :] * lru_buf[0:rows, :]
    for k in range(1, LRU_CONV):
        xc = xc + w_lc_ref[k:k + 1, :] * lru_buf[k * nb:k * nb + rows, :]
    xc_b = xc.astype(_BF16)
    ffn_done = _zero_after(run_ffn(2, _zero_after(xl)))

    def block_diag_dot(w_ref):
        half = GATE_BLOCK // 2
        return jnp.concatenate(
            [_dot(xc_b[:, _mxu_block(g)], w_ref[g * half:(g + 1) * half, :])
             for g in range(D_LRU // GATE_BLOCK)], axis=-1)

    r = _sigmoid(block_diag_dot(w_r_ref) + b_r_ref[...])
    i = _sigmoid(block_diag_dot(w_i_ref) + (b_i_ref[...] + _widen(ffn_done, D_LRU)))
    neg_lam = -lam_ref[...]
    softplus = jnp.maximum(neg_lam, 0.0) + jnp.log1p(jnp.exp(-jnp.abs(neg_lam)))
    a = jnp.exp((-RG_C * softplus) * r)
    u = jnp.sqrt(1.0 - a * a) * (i * xc)
    ffn_done = _zero_after(run_ffn(2, _zero_after(xc)))

    carry = h_state[...]
    frames = []
    for t in range(tt):
        carry = a[t * nb:(t + 1) * nb, :] * carry + u[t * nb:(t + 1) * nb, :]
        frames.append(carry)
    h_state[...] = carry
    hs = jnp.concatenate(frames, axis=0)

    glu = ((_dot(h, w_in_ref[:, 2 * D_MODEL:3 * D_MODEL])
            + (b_in_ref[:, 2 * D_MODEL:3 * D_MODEL] + _widen(ffn_done, D_MODEL)))
           * _sigmoid(proj(3)))
    for j in range(LANE_BLOCKS):
        ccm_buf[j, ccm_hist:ccm_hist + rows, :] = glu[:, _lane_block(j)]

    gated_blocks, lru_out_blocks, s_lru_blocks, s_ccm_blocks = [], [], [], []

    def gate_branch_a(n, after):
        gl = proj_block(1, n, after)
        gated_blocks.append((jax.nn.gelu(gl) * hs[:, _mxu_block(n)]).astype(_BF16))
        return gl

    def project_branch_a(n, after):
        gated = jnp.concatenate(gated_blocks, axis=-1)
        lru_out_blocks.append(
            _dot(gated, w_lo_ref[:, _mxu_block(n)]) + _widen(after, V7X_MXU_DIM))
        return lru_out_blocks[-1]

    def merge_gate_a(n, after):
        s_lru_blocks.append(_sigmoid(proj_block(4, n, after)))
        return s_lru_blocks[-1]

    def merge_gate_b(n, after):
        s_ccm_blocks.append(_sigmoid(proj_block(5, n, after)))
        return s_ccm_blocks[-1]

    side_work = [functools.partial(f, n)
                 for f in (gate_branch_a, project_branch_a, merge_gate_a, merge_gate_b)
                 for n in range(MXU_BLOCKS)]
    side_per_block = len(side_work) // LANE_BLOCKS
    ffn_during_conv = len(ffn_work) - MXU_BLOCKS
    ffn_per_block = -(-ffn_during_conv // LANE_BLOCKS)

    d_blocks = []
    conv_done = _zero_after(glu)
    side_done = conv_done
    for j in range(LANE_BLOCKS):
        cols = _lane_block(j)
        dj = (b_dw_ref[:, cols] + side_done) + w_dw_ref[0:1, cols] * ccm_buf[j, 0:rows, :]
        for k in range(1, CCM_KERNEL):
            dj = dj + w_dw_ref[k:k + 1, cols] * ccm_buf[j, k * nb:k * nb + rows, :]
        d_blocks.append(dj)
        for work in side_work[j * side_per_block:(j + 1) * side_per_block]:
            last = work(conv_done)
        n_ffn = min(ffn_per_block, len(ffn_work) - MXU_BLOCKS)
        if n_ffn:
            last = run_ffn(n_ffn, conv_done)
        conv_done = _zero_after(
            jnp.max(dj.reshape(rows // V7X_SUBLANES, V7X_SUBLANES, V7X_LANES), axis=0))
        side_done = _zero_after(last[rows - V7X_SUBLANES:rows, :])
    d = jnp.concatenate(d_blocks, axis=-1)
    lru_out = jnp.concatenate(lru_out_blocks, axis=-1)

    ffn_done = _zero_after(run_ffn(MXU_BLOCKS // 2, conv_done))
    mu = jnp.mean(d, axis=-1, keepdims=True)
    dc = d - mu
    var = jnp.mean(dc * dc, axis=-1, keepdims=True)
    dn = dc * lax.rsqrt(var + EPS) * g_ln_ref[...] + (b_ln_ref[...] + _widen(side_done, D_CONV))
    dn = dn * _sigmoid(dn)
    ccm_out = _dot(dn.astype(_BF16), w_co_ref[...])

    merged = (jnp.concatenate(s_lru_blocks, axis=-1) * lru_out
              + jnp.concatenate(s_ccm_blocks, axis=-1) * ccm_out)
    run_ffn(len(ffn_work), _zero_after(ccm_out))
    x1 = x + (_dot(merged.astype(_BF16), w_out_ref[...]) + _widen(ffn_done, D_MODEL))
    x1_buf[...] = x1

    y_tm = _rms_norm(jnp.concatenate(x2_blocks, axis=-1), g_final_ref[...])
    for j in range(LANE_BLOCKS):
        perm_buf[j] = y_tm[:, _lane_block(j)]

    def write_y(y_ref):
        for b in range(nb):
            y_ref[b] = jnp.concatenate(
                [perm_buf[j, pl.ds(b, tt, stride=nb), :] for j in range(LANE_BLOCKS)], axis=-1)

    pl.when((step >= 1) & (step <= tiling.n_prompt))(functools.partial(write_y, yp_ref))
    pl.when(step > tiling.n_prompt)(functools.partial(write_y, ys_ref))

    lru_tail = lru_buf[rows:rows + lru_hist, :]
    lru_buf[0:lru_hist, :] = lru_tail
    ccm_tails = []
    for j in range(LANE_BLOCKS):
        ccm_tail = ccm_buf[j, rows:rows + ccm_hist, :]
        ccm_buf[j, 0:ccm_hist, :] = ccm_tail
        ccm_tails.append(ccm_tail)

    @pl.when(tiling.is_group_end(step))
    def _write_final_state():
        state_out_ref[0:nb, :] = h_state[...]
        state_out_ref[nb:nb + lru_hist, :] = lru_tail
        state_out_ref[nb + lru_hist:state_rows, :] = jnp.concatenate(ccm_tails, axis=-1)


def _resident(shape):
    return pl.BlockSpec(shape, lambda s: (0,) * len(shape), pipeline_mode=pl.Buffered(1))


def _nbytes(shape, dtype):
    n = jnp.dtype(dtype).itemsize
    for s in shape:
        n *= s
    return n


def _layer_call(x_prompt, x_sample, state, params):
    tiling = _Tiling(x_prompt.shape[1], x_sample.shape[0], x_sample.shape[1])
    assert x_prompt.shape[0] == GROUP_STREAMS
    nb, tt = GROUP_STREAMS, tiling.tt
    rows = nb * tt
    lru_hist = (LRU_CONV - 1) * nb
    ccm_hist = (CCM_KERNEL - 1) * nb
    d_ff = params[-4].shape[1]
    n_p, n_s = tiling.n_prompt, tiling.n_sample
    tile = (nb, tt, D_MODEL)
    state_shape = (STATE_FRAMES * nb, D_MODEL)
    assert D_LRU == D_MODEL and D_CONV == D_MODEL and state.shape[1:] == state_shape

    def sample_block(q):
        i = tiling.sample_index(q)
        return (i // n_s, i % n_s, 0)

    x_specs = [pl.BlockSpec(tile, lambda q: (0, jnp.clip(q, 0, n_p - 1), 0)),
               pl.BlockSpec(tile, sample_block)]
    y_specs = [pl.BlockSpec(tile, lambda q: (0, jnp.clip(q - 1, 0, n_p - 1), 0)),
               pl.BlockSpec(tile, lambda q: sample_block(q - 1))]
    state_spec = pl.BlockSpec((None,) + state_shape, lambda q: (tiling.group(q), 0, 0))
    row_tile_bytes = _nbytes((rows, D_MODEL), _F32)
    scratch = [
        ((lru_hist + rows, D_LRU), _F32),
        ((LANE_BLOCKS, ccm_hist + rows, V7X_LANES), _F32),
        ((nb, D_LRU), _F32),
        ((LANE_BLOCKS, rows, V7X_LANES), _F32),
        ((rows, D_MODEL), _F32),
    ]
    need = (sum(_nbytes(a.shape, a.dtype) for a in params)
            + 4 * _nbytes(state_shape, _F32)
            + 8 * row_tile_bytes
            + sum(_nbytes(s, d) for s, d in scratch)
            + 12 * row_tile_bytes + 2 * _nbytes((rows, d_ff), _F32))
    assert need <= V7X_VMEM_BYTES, need
    n_groups = tiling.n_groups
    return pl.pallas_call(
        functools.partial(_layer_kernel, tiling=tiling),
        grid=(tiling.n_tiles + 1,),
        in_specs=x_specs + [state_spec] + [_resident(a.shape) for a in params],
        out_specs=y_specs + [state_spec],
        out_shape=[jax.ShapeDtypeStruct(x_prompt.shape, _F32),
                   jax.ShapeDtypeStruct(x_sample.shape, _F32),
                   jax.ShapeDtypeStruct((n_groups,) + state_shape, _F32)],
        scratch_shapes=[pltpu.VMEM(s, d) for s, d in scratch],
        compiler_params=pltpu.CompilerParams(
            dimension_semantics=("arbitrary",), vmem_limit_bytes=int(need)),
        name="layer",
    )(x_prompt, x_sample, state, *params)


def _pack_kernel(*refs, row_blocks):
    step = pl.program_id(0)
    w_refs, o_refs = refs[:len(row_blocks)], refs[len(row_blocks):]
    for w_ref, o_ref, n_blocks in zip(w_refs, o_refs, row_blocks):
        @pl.when(step < n_blocks)
        def _pack(w_ref=w_ref, o_ref=o_ref):
            o_ref[...] = pltpu.bitcast(w_ref[...].astype(_BF16), jnp.uint32)


def _pack_rows(weights):
    row_blocks = tuple(w.shape[0] // PACK_ROWS for w in weights)
    assert all(w.shape[0] % PACK_ROWS == 0 for w in weights)

    def spec(w, rows, last):
        return pl.BlockSpec((rows, w.shape[1]), lambda s: (jnp.minimum(s, last), 0))

    block_bytes = sum(_nbytes((PACK_ROWS, w.shape[1]), _F32) for w in weights)
    return pl.pallas_call(
        functools.partial(_pack_kernel, row_blocks=row_blocks),
        grid=(max(row_blocks),),
        in_specs=[spec(w, PACK_ROWS, n - 1) for w, n in zip(weights, row_blocks)],
        out_specs=[spec(w, PACK_ROWS // 2, n - 1) for w, n in zip(weights, row_blocks)],
        out_shape=[jax.ShapeDtypeStruct((w.shape[0] // 2, w.shape[1]), jnp.uint32)
                   for w in weights],
        compiler_params=pltpu.CompilerParams(
            dimension_semantics=("arbitrary",),
            vmem_limit_bytes=4 * block_bytes),
        name="pack_rows",
    )(*weights)


def _gate_head_blocks(w):
    per = GATE_BLOCK // LRU_HEAD_DIM
    w = w.reshape(LRU_HEADS // per, per, LRU_HEAD_DIM, LRU_HEAD_DIM)
    eye = jnp.eye(per, dtype=w.dtype)
    return jnp.einsum("gpde,pq->gpdqe", w, eye).reshape(D_LRU, GATE_BLOCK)


def _group_time_major(a):
    streams, frames, c = a.shape
    a = a.reshape(streams // GROUP_STREAMS, GROUP_STREAMS, frames, c)
    return jnp.swapaxes(a, 1, 2).reshape(streams // GROUP_STREAMS, frames * GROUP_STREAMS, c)


def _group_stream_major(a):
    groups, rows, c = a.shape
    a = a.reshape(groups, rows // GROUP_STREAMS, GROUP_STREAMS, c)
    return jnp.swapaxes(a, 1, 2).reshape(groups * GROUP_STREAMS, rows // GROUP_STREAMS, c)


def kernel(x_prompt, x_sample, state_lru_h, cache_lru_conv, cache_ccm_conv, g_mix, w_in, b_in, w_lru_conv, b_lru_conv, w_rg_r, b_rg_r, w_rg_i, b_rg_i, lru_lambda, w_lru_o, w_ccm_dw, b_ccm_dw, g_ccm_ln, b_ccm_ln, w_ccm_o, w_out, g_ffn, w_ffn_gate, w_ffn_up, w_ffn_down, g_final):
    assert g_mix.shape[0] == 1, "single-layer trunk"
    row = lambda v: v.reshape(1, -1)
    (p_in, p_r, p_i, p_lo, p_co, p_out, p_gate, p_up, p_down) = _pack_rows([
        w_in[0], _gate_head_blocks(w_rg_r[0]), _gate_head_blocks(w_rg_i[0]), w_lru_o[0],
        w_ccm_o[0], w_out[0], w_ffn_gate[0], w_ffn_up[0], w_ffn_down[0]])
    params = (
        row(g_mix[0]), p_in, row(b_in[0]), w_lru_conv[0], row(b_lru_conv[0]),
        p_r, row(b_rg_r[0]), p_i, row(b_rg_i[0]),
        row(lru_lambda[0]), p_lo,
        w_ccm_dw[0], row(b_ccm_dw[0]), row(g_ccm_ln[0]), row(b_ccm_ln[0]),
        p_co, p_out,
        row(g_ffn[0]), p_gate, p_up, p_down, row(g_final))

    bp = x_prompt.shape[0]
    sample_state = jnp.concatenate(
        [state_lru_h[0][:, None, :], cache_lru_conv[0], cache_ccm_conv[0]], axis=1)
    prompt_state = jnp.zeros((bp,) + sample_state.shape[1:], x_prompt.dtype)
    state = _group_time_major(jnp.concatenate([prompt_state, sample_state], axis=0))

    y_p, y_s, state = _layer_call(x_prompt, x_sample, state, params)
    state = _group_stream_major(state)
    h_last, lru_tail, ccm_tail = state[:, 0], state[:, 1:LRU_CONV], state[:, LRU_CONV:]
    return (y_p, y_s, h_last[None, :bp], lru_tail[None, :bp], ccm_tail[None, :bp],
            h_last[None, bp:], lru_tail[None, bp:], ccm_tail[None, bp:])
```

```python
import functools

import jax
import jax.numpy as jnp
from jax import lax
from jax.experimental import pallas as pl
from jax.experimental.pallas import tpu as pltpu

D_MODEL = 1024
D_LRU = 1024
D_CONV = 1024
LRU_HEADS = 16
LRU_HEAD_DIM = D_LRU // LRU_HEADS
LRU_CONV = 4
CCM_KERNEL = 31
RG_C = 8.0
EPS = 1e-6

V7X_MXU_DIM = 256
V7X_LANES = 128
V7X_SUBLANES = 8
V7X_VMEM_BYTES = 64 * 1024 * 1024
GATE_BLOCK = V7X_MXU_DIM
LANE_BLOCKS = D_MODEL // V7X_LANES
MXU_BLOCKS = D_MODEL // V7X_MXU_DIM
GROUP_STREAMS = V7X_SUBLANES
ROW_TILE = 128
PACK_ROWS = 128
STATE_FRAMES = 1 + (LRU_CONV - 1) + (CCM_KERNEL - 1)
FFN_CHUNKS_BEFORE_CONV = 5

_F32 = jnp.float32
_BF16 = jnp.bfloat16


def _dot(a, w_packed):
    return jnp.dot(a, pltpu.bitcast(w_packed, _BF16), preferred_element_type=_F32)


def _sigmoid(x):
    return 0.5 * jnp.tanh(0.5 * x) + 0.5


def _rms_norm(x, g):
    return x * lax.rsqrt(jnp.mean(x * x, axis=-1, keepdims=True) + EPS) * g


def _lane_block(j):
    return slice(j * V7X_LANES, (j + 1) * V7X_LANES)


def _mxu_block(n):
    return slice(n * V7X_MXU_DIM, (n + 1) * V7X_MXU_DIM)


def _zero_after(value):
    bits = value[0:1, 0:V7X_LANES].astype(jnp.int32)
    zero = lax.shift_right_logical(lax.shift_right_logical(bits, 16), 16)
    return zero.astype(_F32)


def _widen(zero_row, width):
    return jnp.concatenate([zero_row] * (width // V7X_LANES), axis=-1)


class _Tiling:
    def __init__(self, prompt_seq, sample_streams, sample_seq):
        self.tt = ROW_TILE // GROUP_STREAMS
        assert prompt_seq % self.tt == 0 and sample_seq % self.tt == 0
        assert sample_streams % GROUP_STREAMS == 0
        self.n_prompt = prompt_seq // self.tt
        self.n_sample = sample_seq // self.tt
        self.sample_groups = sample_streams // GROUP_STREAMS
        self.n_tiles = self.n_prompt + self.sample_groups * self.n_sample
        self.n_groups = 1 + self.sample_groups

    def sample_index(self, q):
        return jnp.clip(q - self.n_prompt, 0, self.sample_groups * self.n_sample - 1)

    def group(self, q):
        q = jnp.clip(q, 0, self.n_tiles - 1)
        return jnp.where(q < self.n_prompt, 0, 1 + self.sample_index(q) // self.n_sample)

    def is_group_start(self, q):
        return (q == 0) | ((q >= self.n_prompt) & (q < self.n_tiles)
                           & (self.sample_index(q) % self.n_sample == 0))

    def is_group_end(self, q):
        return ((q == self.n_prompt - 1)
                | ((q >= self.n_prompt) & (q < self.n_tiles)
                   & (self.sample_index(q) % self.n_sample == self.n_sample - 1)))


def _layer_kernel(xp_ref, xs_ref, state_ref,
                  g_mix_ref, w_in_ref, b_in_ref, w_lc_ref, b_lc_ref,
                  w_r_ref, b_r_ref, w_i_ref, b_i_ref, lam_ref, w_lo_ref,
                  w_dw_ref, b_dw_ref, g_ln_ref, b_ln_ref, w_co_ref, w_out_ref,
                  g_ffn_ref, w_gate_ref, w_up_ref, w_down_ref, g_final_ref,
                  yp_ref, ys_ref, state_out_ref,
                  lru_buf, ccm_buf, h_state, x1_buf, *, tiling):
    step = pl.program_id(0)
    nb, tt = GROUP_STREAMS, tiling.tt
    rows = nb * tt
    lru_hist = (LRU_CONV - 1) * nb
    ccm_hist = (CCM_KERNEL - 1) * nb
    state_rows = STATE_FRAMES * nb
    d_ff = w_gate_ref.shape[1]
    ffn_blocks = d_ff // V7X_MXU_DIM

    @pl.when(step == 0)
    def _clear_ffn_input():
        x1_buf[...] = jnp.zeros_like(x1_buf)

    @pl.when(tiling.is_group_start(step))
    def _load_carried_state():
        h_state[...] = state_ref[0:nb, :]
        lru_buf[0:lru_hist, :] = state_ref[nb:nb + lru_hist, :]
        for j in range(LANE_BLOCKS):
            ccm_buf[j, 0:ccm_hist, :] = state_ref[nb + lru_hist:state_rows, _lane_block(j)]

    x1_prev = x1_buf[...]
    h2 = _rms_norm(x1_prev, g_ffn_ref[...]).astype(_BF16)
    ff_blocks, x2_blocks = [], []

    def ffn_gate_up(n, after):
        cols = _mxu_block(n)
        gate = _dot(h2, w_gate_ref[:, cols]) + _widen(after, V7X_MXU_DIM)
        up = _dot(h2, w_up_ref[:, cols])
        ff_blocks.append((gate * _sigmoid(gate) * up).astype(_BF16))
        return up

    def ffn_down(n, after):
        cols = _mxu_block(n)
        ff = jnp.concatenate(ff_blocks, axis=-1)
        down = _dot(ff, w_down_ref[:, cols]) + _widen(after, V7X_MXU_DIM)
        x2_blocks.append(x1_prev[:, cols] + down)
        return down

    ffn_work = ([functools.partial(ffn_gate_up, n) for n in range(ffn_blocks)]
                + [functools.partial(ffn_down, n) for n in range(MXU_BLOCKS)])

    def run_ffn(count, after):
        last = None
        for _ in range(count):
            last = ffn_work.pop(0)(after)
        return last

    no_wait = jnp.zeros((1, V7X_LANES), _F32)

    from_prompt = step < tiling.n_prompt
    x_sm = jnp.where(from_prompt, xp_ref[...], xs_ref[...])
    x = pltpu.einshape("btc->tbc", x_sm).reshape(rows, D_MODEL)
    h = _rms_norm(x, g_mix_ref[...]).astype(_BF16)

    def proj(split):
        cols = slice(split * D_MODEL, (split + 1) * D_MODEL)
        return _dot(h, w_in_ref[:, cols]) + b_in_ref[:, cols]

    def proj_block(split, n, after):
        c0 = split * D_MODEL + n * V7X_MXU_DIM
        cols = slice(c0, c0 + V7X_MXU_DIM)
        return _dot(h, w_in_ref[:, cols]) + (b_in_ref[:, cols] + _widen(after, V7X_MXU_DIM))

    xl = proj(0)
    lru_buf[lru_hist:lru_hist + rows, :] = xl
    run_ffn(FFN_CHUNKS_BEFORE_CONV, no_wait)
    lru_half = D_LRU // 2
    hs_halves = []
    for c0 in (0, lru_half):
        cols = slice(c0, c0 + lru_half)
        xc = b_lc_ref[:, cols] + w_lc_ref[0:1, cols] * lru_buf[0:rows, cols]
        for k in range(1, LRU_CONV):
            xc = xc + w_lc_ref[k:k + 1, cols] * lru_buf[k * nb:k * nb + rows, cols]
        xc_b = xc.astype(_BF16)

        def block_diag_dot(w_ref, xc_b=xc_b, c0=c0):
            half = GATE_BLOCK // 2
            g0 = c0 // GATE_BLOCK
            return jnp.concatenate(
                [_dot(xc_b[:, _mxu_block(g)], w_ref[(g0 + g) * half:(g0 + g + 1) * half, :])
                 for g in range(lru_half // GATE_BLOCK)], axis=-1)

        r = _sigmoid(block_diag_dot(w_r_ref) + b_r_ref[:, cols])
        i = _sigmoid(block_diag_dot(w_i_ref) + b_i_ref[:, cols])
        neg_lam = -lam_ref[:, cols]
        softplus = jnp.maximum(neg_lam, 0.0) + jnp.log1p(jnp.exp(-jnp.abs(neg_lam)))
        a = jnp.exp((-RG_C * softplus) * r)
        u = jnp.sqrt(1.0 - a * a) * (i * xc)

        carry = h_state[:, cols]
        frames = []
        for t in range(tt):
            carry = a[t * nb:(t + 1) * nb, :] * carry + u[t * nb:(t + 1) * nb, :]
            frames.append(carry)
        h_state[:, cols] = carry
        hs_halves.append(jnp.concatenate(frames, axis=0))
    hs = jnp.concatenate(hs_halves, axis=-1)

    glu = ((_dot(h, w_in_ref[:, 2 * D_MODEL:3 * D_MODEL])
            + b_in_ref[:, 2 * D_MODEL:3 * D_MODEL])
           * _sigmoid(proj(3)))
    for j in range(LANE_BLOCKS):
        ccm_buf[j, ccm_hist:ccm_hist + rows, :] = glu[:, _lane_block(j)]

    gated_blocks, lru_out_blocks, s_lru_blocks, s_ccm_blocks = [], [], [], []

    def gate_branch_a(n, after):
        gl = proj_block(1, n, after)
        gated_blocks.append((jax.nn.gelu(gl) * hs[:, _mxu_block(n)]).astype(_BF16))
        return gl

    def project_branch_a(n, after):
        gated = jnp.concatenate(gated_blocks, axis=-1)
        lru_out_blocks.append(
            _dot(gated, w_lo_ref[:, _mxu_block(n)]) + _widen(after, V7X_MXU_DIM))
        return lru_out_blocks[-1]

    def merge_gate_a(n, after):
        s_lru_blocks.append(_sigmoid(proj_block(4, n, after)))
        return s_lru_blocks[-1]

    def merge_gate_b(n, after):
        s_ccm_blocks.append(_sigmoid(proj_block(5, n, after)))
        return s_ccm_blocks[-1]

    side_work = [functools.partial(f, n)
                 for f in (gate_branch_a, project_branch_a, merge_gate_a, merge_gate_b)
                 for n in range(MXU_BLOCKS)]
    side_per_block = len(side_work) // LANE_BLOCKS
    ffn_during_conv = len(ffn_work) - MXU_BLOCKS
    ffn_per_block = -(-ffn_during_conv // LANE_BLOCKS)

    d_blocks = []
    conv_done = _zero_after(glu)
    side_done = conv_done
    for j in range(LANE_BLOCKS):
        cols = _lane_block(j)
        dj = (b_dw_ref[:, cols] + side_done) + w_dw_ref[0:1, cols] * ccm_buf[j, 0:rows, :]
        for k in range(1, CCM_KERNEL):
            dj = dj + w_dw_ref[k:k + 1, cols] * ccm_buf[j, k * nb:k * nb + rows, :]
        d_blocks.append(dj)
        for work in side_work[j * side_per_block:(j + 1) * side_per_block]:
            last = work(conv_done)
        n_ffn = min(ffn_per_block, len(ffn_work) - MXU_BLOCKS)
        if n_ffn:
            last = run_ffn(n_ffn, conv_done)
        conv_done = _zero_after(
            jnp.max(dj.reshape(rows // V7X_SUBLANES, V7X_SUBLANES, V7X_LANES), axis=0))
        side_done = _zero_after(last[rows - V7X_SUBLANES:rows, :])
    d = jnp.concatenate(d_blocks, axis=-1)
    lru_out = jnp.concatenate(lru_out_blocks, axis=-1)

    ffn_done = _zero_after(run_ffn(MXU_BLOCKS // 2, conv_done))
    mu = jnp.mean(d, axis=-1, keepdims=True)
    dc = d - mu
    var = jnp.mean(dc * dc, axis=-1, keepdims=True)
    dn = dc * lax.rsqrt(var + EPS) * g_ln_ref[...] + (b_ln_ref[...] + _widen(side_done, D_CONV))
    dn = dn * _sigmoid(dn)
    ccm_out = _dot(dn.astype(_BF16), w_co_ref[...])

    merged = (jnp.concatenate(s_lru_blocks, axis=-1) * lru_out
              + jnp.concatenate(s_ccm_blocks, axis=-1) * ccm_out)
    run_ffn(len(ffn_work), _zero_after(ccm_out))
    x1 = x + (_dot(merged.astype(_BF16), w_out_ref[...]) + _widen(ffn_done, D_MODEL))
    x1_buf[...] = x1

    y_tm = _rms_norm(jnp.concatenate(x2_blocks, axis=-1), g_final_ref[...])
    y_sm = pltpu.einshape("tbc->btc", y_tm.reshape(tt, nb, D_MODEL))

    def write_y(y_ref):
        y_ref[...] = y_sm

    pl.when((step >= 1) & (step <= tiling.n_prompt))(functools.partial(write_y, yp_ref))
    pl.when(step > tiling.n_prompt)(functools.partial(write_y, ys_ref))

    lru_tail = lru_buf[rows:rows + lru_hist, :]
    lru_buf[0:lru_hist, :] = lru_tail
    ccm_tails = []
    for j in range(LANE_BLOCKS):
        ccm_tail = ccm_buf[j, rows:rows + ccm_hist, :]
        ccm_buf[j, 0:ccm_hist, :] = ccm_tail
        ccm_tails.append(ccm_tail)

    @pl.when(tiling.is_group_end(step))
    def _write_final_state():
        state_out_ref[0:nb, :] = h_state[...]
        state_out_ref[nb:nb + lru_hist, :] = lru_tail
        state_out_ref[nb + lru_hist:state_rows, :] = jnp.concatenate(ccm_tails, axis=-1)


def _resident(shape):
    return pl.BlockSpec(shape, lambda s: (0,) * len(shape), pipeline_mode=pl.Buffered(1))


def _nbytes(shape, dtype):
    n = jnp.dtype(dtype).itemsize
    for s in shape:
        n *= s
    return n


def _layer_call(x_prompt, x_sample, state, params):
    tiling = _Tiling(x_prompt.shape[1], x_sample.shape[0], x_sample.shape[1])
    assert x_prompt.shape[0] == GROUP_STREAMS
    nb, tt = GROUP_STREAMS, tiling.tt
    rows = nb * tt
    lru_hist = (LRU_CONV - 1) * nb
    ccm_hist = (CCM_KERNEL - 1) * nb
    d_ff = params[-4].shape[1]
    n_p, n_s = tiling.n_prompt, tiling.n_sample
    tile = (nb, tt, D_MODEL)
    state_shape = (STATE_FRAMES * nb, D_MODEL)
    assert D_LRU == D_MODEL and D_CONV == D_MODEL and state.shape[1:] == state_shape

    def sample_block(q):
        i = tiling.sample_index(q)
        return (i // n_s, i % n_s, 0)

    x_specs = [pl.BlockSpec(tile, lambda q: (0, jnp.clip(q, 0, n_p - 1), 0)),
               pl.BlockSpec(tile, sample_block)]
    y_specs = [pl.BlockSpec(tile, lambda q: (0, jnp.clip(q - 1, 0, n_p - 1), 0)),
               pl.BlockSpec(tile, lambda q: sample_block(q - 1))]
    state_spec = pl.BlockSpec((None,) + state_shape, lambda q: (tiling.group(q), 0, 0))
    row_tile_bytes = _nbytes((rows, D_MODEL), _F32)
    scratch = [
        ((lru_hist + rows, D_LRU), _F32),
        ((LANE_BLOCKS, ccm_hist + rows, V7X_LANES), _F32),
        ((nb, D_LRU), _F32),
        ((rows, D_MODEL), _F32),
    ]
    need = (sum(_nbytes(a.shape, a.dtype) for a in params)
            + 4 * _nbytes(state_shape, _F32)
            + 8 * row_tile_bytes
            + sum(_nbytes(s, d) for s, d in scratch)
            + 12 * row_tile_bytes + 2 * _nbytes((rows, d_ff), _F32))
    assert need <= V7X_VMEM_BYTES, need
    n_groups = tiling.n_groups
    return pl.pallas_call(
        functools.partial(_layer_kernel, tiling=tiling),
        grid=(tiling.n_tiles + 1,),
        in_specs=x_specs + [state_spec] + [_resident(a.shape) for a in params],
        out_specs=y_specs + [state_spec],
        out_shape=[jax.ShapeDtypeStruct(x_prompt.shape, _F32),
                   jax.ShapeDtypeStruct(x_sample.shape, _F32),
                   jax.ShapeDtypeStruct((n_groups,) + state_shape, _F32)],
        scratch_shapes=[pltpu.VMEM(s, d) for s, d in scratch],
        compiler_params=pltpu.CompilerParams(
            dimension_semantics=("arbitrary",), vmem_limit_bytes=int(need)),
        name="layer",
    )(x_prompt, x_sample, state, *params)


def _pack_kernel(*refs, row_blocks):
    step = pl.program_id(0)
    w_refs, o_refs = refs[:len(row_blocks)], refs[len(row_blocks):]
    for w_ref, o_ref, n_blocks in zip(w_refs, o_refs, row_blocks):
        @pl.when(step < n_blocks)
        def _pack(w_ref=w_ref, o_ref=o_ref):
            o_ref[...] = pltpu.bitcast(w_ref[...].astype(_BF16), jnp.uint32)


def _pack_rows(weights):
    row_blocks = tuple(w.shape[0] // PACK_ROWS for w in weights)
    assert all(w.shape[0] % PACK_ROWS == 0 for w in weights)

    def spec(w, rows, last):
        return pl.BlockSpec((rows, w.shape[1]), lambda s: (jnp.minimum(s, last), 0))

    block_bytes = sum(_nbytes((PACK_ROWS, w.shape[1]), _F32) for w in weights)
    return pl.pallas_call(
        functools.partial(_pack_kernel, row_blocks=row_blocks),
        grid=(max(row_blocks),),
        in_specs=[spec(w, PACK_ROWS, n - 1) for w, n in zip(weights, row_blocks)],
        out_specs=[spec(w, PACK_ROWS // 2, n - 1) for w, n in zip(weights, row_blocks)],
        out_shape=[jax.ShapeDtypeStruct((w.shape[0] // 2, w.shape[1]), jnp.uint32)
                   for w in weights],
        compiler_params=pltpu.CompilerParams(
            dimension_semantics=("arbitrary",),
            vmem_limit_bytes=4 * block_bytes),
        name="pack_rows",
    )(*weights)


def _gate_head_blocks(w):
    per = GATE_BLOCK // LRU_HEAD_DIM
    w = w.reshape(LRU_HEADS // per, per, LRU_HEAD_DIM, LRU_HEAD_DIM)
    eye = jnp.eye(per, dtype=w.dtype)
    return jnp.einsum("gpde,pq->gpdqe", w, eye).reshape(D_LRU, GATE_BLOCK)


def _group_time_major(a):
    streams, frames, c = a.shape
    a = a.reshape(streams // GROUP_STREAMS, GROUP_STREAMS, frames, c)
    return jnp.swapaxes(a, 1, 2).reshape(streams // GROUP_STREAMS, frames * GROUP_STREAMS, c)


def _group_stream_major(a):
    groups, rows, c = a.shape
    a = a.reshape(groups, rows // GROUP_STREAMS, GROUP_STREAMS, c)
    return jnp.swapaxes(a, 1, 2).reshape(groups * GROUP_STREAMS, rows // GROUP_STREAMS, c)


def kernel(x_prompt, x_sample, state_lru_h, cache_lru_conv, cache_ccm_conv, g_mix, w_in, b_in, w_lru_conv, b_lru_conv, w_rg_r, b_rg_r, w_rg_i, b_rg_i, lru_lambda, w_lru_o, w_ccm_dw, b_ccm_dw, g_ccm_ln, b_ccm_ln, w_ccm_o, w_out, g_ffn, w_ffn_gate, w_ffn_up, w_ffn_down, g_final):
    assert g_mix.shape[0] == 1, "single-layer trunk"
    row = lambda v: v.reshape(1, -1)
    (p_in, p_r, p_i, p_lo, p_co, p_out, p_gate, p_up, p_down) = _pack_rows([
        w_in[0], _gate_head_blocks(w_rg_r[0]), _gate_head_blocks(w_rg_i[0]), w_lru_o[0],
        w_ccm_o[0], w_out[0], w_ffn_gate[0], w_ffn_up[0], w_ffn_down[0]])
    params = (
        row(g_mix[0]), p_in, row(b_in[0]), w_lru_conv[0], row(b_lru_conv[0]),
        p_r, row(b_rg_r[0]), p_i, row(b_rg_i[0]),
        row(lru_lambda[0]), p_lo,
        w_ccm_dw[0], row(b_ccm_dw[0]), row(g_ccm_ln[0]), row(b_ccm_ln[0]),
        p_co, p_out,
        row(g_ffn[0]), p_gate, p_up, p_down, row(g_final))

    bp = x_prompt.shape[0]
    sample_state = jnp.concatenate(
        [state_lru_h[0][:, None, :], cache_lru_conv[0], cache_ccm_conv[0]], axis=1)
    prompt_state = jnp.zeros((bp,) + sample_state.shape[1:], x_prompt.dtype)
    state = _group_time_major(jnp.concatenate([prompt_state, sample_state], axis=0))

    y_p, y_s, state = _layer_call(x_prompt, x_sample, state, params)
    state = _group_stream_major(state)
    h_last, lru_tail, ccm_tail = state[:, 0], state[:, 1:LRU_CONV], state[:, LRU_CONV:]
    return (y_p, y_s, h_last[None, :bp], lru_tail[None, :bp], ccm_tail[None, :bp],
            h_last[None, bp:], lru_tail[None, bp:], ccm_tail[None, bp:])
```

```python
import functools

import jax
import jax.numpy as jnp
from jax import lax
from jax.experimental import pallas as pl
from jax.experimental.pallas import tpu as pltpu

D_MODEL = 1024
D_LRU = 1024
D_CONV = 1024
LRU_HEADS = 16
LRU_HEAD_DIM = D_LRU // LRU_HEADS
LRU_CONV = 4
CCM_KERNEL = 31
RG_C = 8.0
EPS = 1e-6

V7X_MXU_DIM = 256
V7X_LANES = 128
V7X_SUBLANES = 8
V7X_VMEM_BYTES = 64 * 1024 * 1024
GATE_BLOCK = V7X_MXU_DIM
LANE_BLOCKS = D_MODEL // V7X_LANES
MXU_BLOCKS = D_MODEL // V7X_MXU_DIM
GROUP_STREAMS = V7X_SUBLANES
ROW_TILE = 128
PACK_ROWS = 128
STATE_FRAMES = 1 + (LRU_CONV - 1) + (CCM_KERNEL - 1)
FFN_CHUNKS_BEFORE_CONV = 5

_F32 = jnp.float32
_BF16 = jnp.bfloat16


def _dot(a, w_packed):
    return jnp.dot(a, pltpu.bitcast(w_packed, _BF16), preferred_element_type=_F32)


def _sigmoid(x):
    return 0.5 * jnp.tanh(0.5 * x) + 0.5


def _rms_norm(x, g):
    return x * lax.rsqrt(jnp.mean(x * x, axis=-1, keepdims=True) + EPS) * g


def _lane_block(j):
    return slice(j * V7X_LANES, (j + 1) * V7X_LANES)


def _mxu_block(n):
    return slice(n * V7X_MXU_DIM, (n + 1) * V7X_MXU_DIM)


def _zero_after(value):
    bits = value[0:1, 0:V7X_LANES].astype(jnp.int32)
    zero = lax.shift_right_logical(lax.shift_right_logical(bits, 16), 16)
    return zero.astype(_F32)


def _widen(zero_row, width):
    return jnp.concatenate([zero_row] * (width // V7X_LANES), axis=-1)


class _Tiling:
    def __init__(self, prompt_seq, sample_streams, sample_seq):
        self.tt = ROW_TILE // GROUP_STREAMS
        assert prompt_seq % self.tt == 0 and sample_seq % self.tt == 0
        assert sample_streams % GROUP_STREAMS == 0
        self.n_prompt = prompt_seq // self.tt
        self.n_sample = sample_seq // self.tt
        self.sample_groups = sample_streams // GROUP_STREAMS
        self.n_tiles = self.n_prompt + self.sample_groups * self.n_sample
        self.n_groups = 1 + self.sample_groups

    def sample_index(self, q):
        return jnp.clip(q - self.n_prompt, 0, self.sample_groups * self.n_sample - 1)

    def group(self, q):
        q = jnp.clip(q, 0, self.n_tiles - 1)
        return jnp.where(q < self.n_prompt, 0, 1 + self.sample_index(q) // self.n_sample)

    def is_group_start(self, q):
        return (q == 0) | ((q >= self.n_prompt) & (q < self.n_tiles)
                           & (self.sample_index(q) % self.n_sample == 0))

    def is_group_end(self, q):
        return ((q == self.n_prompt - 1)
                | ((q >= self.n_prompt) & (q < self.n_tiles)
                   & (self.sample_index(q) % self.n_sample == self.n_sample - 1)))


def _layer_kernel(xp_ref, xs_ref, state_ref,
                  g_mix_ref, w_in_ref, b_in_ref, w_lc_ref, b_lc_ref,
                  w_r_ref, b_r_ref, w_i_ref, b_i_ref, lam_ref, w_lo_ref,
                  w_dw_ref, b_dw_ref, g_ln_ref, b_ln_ref, w_co_ref, w_out_ref,
                  g_ffn_ref, w_gate_ref, w_up_ref, w_down_ref, g_final_ref,
                  yp_ref, ys_ref, state_out_ref,
                  lru_buf, ccm_buf, h_state, x1_buf, *, tiling):
    step = pl.program_id(0)
    nb, tt = GROUP_STREAMS, tiling.tt
    rows = nb * tt
    lru_hist = (LRU_CONV - 1) * nb
    ccm_hist = (CCM_KERNEL - 1) * nb
    state_rows = STATE_FRAMES * nb
    d_ff = w_gate_ref.shape[1]
    ffn_blocks = d_ff // V7X_MXU_DIM

    @pl.when(step == 0)
    def _clear_ffn_input():
        x1_buf[...] = jnp.zeros_like(x1_buf)

    @pl.when(tiling.is_group_start(step))
    def _load_carried_state():
        h_state[...] = state_ref[0:nb, :]
        lru_buf[0:lru_hist, :] = state_ref[nb:nb + lru_hist, :]
        for j in range(LANE_BLOCKS):
            ccm_buf[j, 0:ccm_hist, :] = state_ref[nb + lru_hist:state_rows, _lane_block(j)]

    x1_prev = x1_buf[...]
    h2 = _rms_norm(x1_prev, g_ffn_ref[...]).astype(_BF16)
    ff_blocks, x2_blocks = [], []

    def ffn_gate_up(n, after):
        cols = _mxu_block(n)
        gate = _dot(h2, w_gate_ref[:, cols]) + _widen(after, V7X_MXU_DIM)
        up = _dot(h2, w_up_ref[:, cols])
        ff_blocks.append((gate * _sigmoid(gate) * up).astype(_BF16))
        return up

    def ffn_down(n, after):
        cols = _mxu_block(n)
        ff = jnp.concatenate(ff_blocks, axis=-1)
        down = _dot(ff, w_down_ref[:, cols]) + _widen(after, V7X_MXU_DIM)
        x2_blocks.append(x1_buf[:, cols] + down)
        return down

    ffn_work = ([functools.partial(ffn_gate_up, n) for n in range(ffn_blocks)]
                + [functools.partial(ffn_down, n) for n in range(MXU_BLOCKS)])

    def run_ffn(count, after):
        last = None
        for _ in range(count):
            last = ffn_work.pop(0)(after)
        return last

    no_wait = jnp.zeros((1, V7X_LANES), _F32)

    from_prompt = step < tiling.n_prompt
    x_sm = jnp.where(from_prompt, xp_ref[...], xs_ref[...])
    x = pltpu.einshape("btc->tbc", x_sm).reshape(rows, D_MODEL)
    h = _rms_norm(x, g_mix_ref[...]).astype(_BF16)

    def proj(split):
        cols = slice(split * D_MODEL, (split + 1) * D_MODEL)
        return _dot(h, w_in_ref[:, cols]) + b_in_ref[:, cols]

    def proj_block(split, n, after):
        c0 = split * D_MODEL + n * V7X_MXU_DIM
        cols = slice(c0, c0 + V7X_MXU_DIM)
        return _dot(h, w_in_ref[:, cols]) + (b_in_ref[:, cols] + _widen(after, V7X_MXU_DIM))

    xl = proj(0)
    lru_buf[lru_hist:lru_hist + rows, :] = xl
    run_ffn(FFN_CHUNKS_BEFORE_CONV, no_wait)
    lru_half = D_LRU // 2
    hs_halves = []
    for c0 in (0, lru_half):
        cols = slice(c0, c0 + lru_half)
        xc = b_lc_ref[:, cols] + w_lc_ref[0:1, cols] * lru_buf[0:rows, cols]
        for k in range(1, LRU_CONV):
            xc = xc + w_lc_ref[k:k + 1, cols] * lru_buf[k * nb:k * nb + rows, cols]
        xc_b = xc.astype(_BF16)

        def block_diag_dot(w_ref, xc_b=xc_b, c0=c0):
            half = GATE_BLOCK // 2
            g0 = c0 // GATE_BLOCK
            return jnp.concatenate(
                [_dot(xc_b[:, _mxu_block(g)], w_ref[(g0 + g) * half:(g0 + g + 1) * half, :])
                 for g in range(lru_half // GATE_BLOCK)], axis=-1)

        r = _sigmoid(block_diag_dot(w_r_ref) + b_r_ref[:, cols])
        i = _sigmoid(block_diag_dot(w_i_ref) + b_i_ref[:, cols])
        neg_lam = -lam_ref[:, cols]
        softplus = jnp.maximum(neg_lam, 0.0) + jnp.log1p(jnp.exp(-jnp.abs(neg_lam)))
        a = jnp.exp((-RG_C * softplus) * r)
        u = jnp.sqrt(1.0 - a * a) * (i * xc)

        carry = h_state[:, cols]
        frames = []
        for t in range(tt):
            carry = a[t * nb:(t + 1) * nb, :] * carry + u[t * nb:(t + 1) * nb, :]
            frames.append(carry)
        h_state[:, cols] = carry
        hs_halves.append(jnp.concatenate(frames, axis=0))
    hs = jnp.concatenate(hs_halves, axis=-1)

    glu = ((_dot(h, w_in_ref[:, 2 * D_MODEL:3 * D_MODEL])
            + b_in_ref[:, 2 * D_MODEL:3 * D_MODEL])
           * _sigmoid(proj(3)))
    for j in range(LANE_BLOCKS):
        ccm_buf[j, ccm_hist:ccm_hist + rows, :] = glu[:, _lane_block(j)]

    gated_blocks, lru_out_blocks, s_lru_blocks, s_ccm_blocks = [], [], [], []

    def gate_branch_a(n, after):
        gl = proj_block(1, n, after)
        gated_blocks.append((jax.nn.gelu(gl) * hs[:, _mxu_block(n)]).astype(_BF16))
        return gl

    def project_branch_a(n, after):
        gated = jnp.concatenate(gated_blocks, axis=-1)
        lru_out_blocks.append(
            _dot(gated, w_lo_ref[:, _mxu_block(n)]) + _widen(after, V7X_MXU_DIM))
        return lru_out_blocks[-1]

    def merge_gate_a(n, after):
        s_lru_blocks.append(_sigmoid(proj_block(4, n, after)))
        return s_lru_blocks[-1]

    def merge_gate_b(n, after):
        s_ccm_blocks.append(_sigmoid(proj_block(5, n, after)))
        return s_ccm_blocks[-1]

    side_work = [functools.partial(f, n)
                 for f in (gate_branch_a, project_branch_a, merge_gate_a, merge_gate_b)
                 for n in range(MXU_BLOCKS)]
    side_per_block = len(side_work) // LANE_BLOCKS
    ffn_during_conv = len(ffn_work) - MXU_BLOCKS
    ffn_per_block = -(-ffn_during_conv // LANE_BLOCKS)

    d_blocks = []
    conv_done = _zero_after(glu)
    side_done = conv_done
    for j in range(LANE_BLOCKS):
        cols = _lane_block(j)
        dj = (b_dw_ref[:, cols] + side_done) + w_dw_ref[0:1, cols] * ccm_buf[j, 0:rows, :]
        for k in range(1, CCM_KERNEL):
            dj = dj + w_dw_ref[k:k + 1, cols] * ccm_buf[j, k * nb:k * nb + rows, :]
        d_blocks.append(dj)
        for work in side_work[j * side_per_block:(j + 1) * side_per_block]:
            last = work(conv_done)
        n_ffn = min(ffn_per_block, len(ffn_work) - MXU_BLOCKS)
        if n_ffn:
            last = run_ffn(n_ffn, conv_done)
        conv_done = _zero_after(
            jnp.max(dj.reshape(rows // V7X_SUBLANES, V7X_SUBLANES, V7X_LANES), axis=0))
        side_done = _zero_after(last[rows - V7X_SUBLANES:rows, :])
    d = jnp.concatenate(d_blocks, axis=-1)
    lru_out = jnp.concatenate(lru_out_blocks, axis=-1)

    ffn_done = _zero_after(run_ffn(MXU_BLOCKS // 2, conv_done))
    mu = jnp.mean(d, axis=-1, keepdims=True)
    dc = d - mu
    var = jnp.mean(dc * dc, axis=-1, keepdims=True)
    dn = dc * lax.rsqrt(var + EPS) * g_ln_ref[...] + (b_ln_ref[...] + _widen(side_done, D_CONV))
    dn = dn * _sigmoid(dn)
    ccm_out = _dot(dn.astype(_BF16), w_co_ref[...])

    merged = (jnp.concatenate(s_lru_blocks, axis=-1) * lru_out
              + jnp.concatenate(s_ccm_blocks, axis=-1) * ccm_out)
    run_ffn(len(ffn_work), _zero_after(ccm_out))
    x1 = x + (_dot(merged.astype(_BF16), w_out_ref[...]) + _widen(ffn_done, D_MODEL))
    x1_buf[...] = x1

    y_tm = _rms_norm(jnp.concatenate(x2_blocks, axis=-1), g_final_ref[...])
    y_sm = pltpu.einshape("tbc->btc", y_tm.reshape(tt, nb, D_MODEL))

    def write_y(y_ref):
        y_ref[...] = y_sm

    pl.when((step >= 1) & (step <= tiling.n_prompt))(functools.partial(write_y, yp_ref))
    pl.when(step > tiling.n_prompt)(functools.partial(write_y, ys_ref))

    lru_tail = lru_buf[rows:rows + lru_hist, :]
    lru_buf[0:lru_hist, :] = lru_tail
    ccm_tails = []
    for j in range(LANE_BLOCKS):
        ccm_tail = ccm_buf[j, rows:rows + ccm_hist, :]
        ccm_buf[j, 0:ccm_hist, :] = ccm_tail
        ccm_tails.append(ccm_tail)

    @pl.when(tiling.is_group_end(step))
    def _write_final_state():
        state_out_ref[0:nb, :] = h_state[...]
        state_out_ref[nb:nb + lru_hist, :] = lru_tail
        state_out_ref[nb + lru_hist:state_rows, :] = jnp.concatenate(ccm_tails, axis=-1)


def _resident(shape):
    return pl.BlockSpec(shape, lambda s: (0,) * len(shape), pipeline_mode=pl.Buffered(1))


def _nbytes(shape, dtype):
    n = jnp.dtype(dtype).itemsize
    for s in shape:
        n *= s
    return n


def _layer_call(x_prompt, x_sample, state, params):
    tiling = _Tiling(x_prompt.shape[1], x_sample.shape[0], x_sample.shape[1])
    assert x_prompt.shape[0] == GROUP_STREAMS
    nb, tt = GROUP_STREAMS, tiling.tt
    rows = nb * tt
    lru_hist = (LRU_CONV - 1) * nb
    ccm_hist = (CCM_KERNEL - 1) * nb
    d_ff = params[-4].shape[1]
    n_p, n_s = tiling.n_prompt, tiling.n_sample
    tile = (nb, tt, D_MODEL)
    state_shape = (STATE_FRAMES * nb, D_MODEL)
    assert D_LRU == D_MODEL and D_CONV == D_MODEL and state.shape[1:] == state_shape

    def sample_block(q):
        i = tiling.sample_index(q)
        return (i // n_s, i % n_s, 0)

    x_specs = [pl.BlockSpec(tile, lambda q: (0, jnp.clip(q, 0, n_p - 1), 0)),
               pl.BlockSpec(tile, sample_block)]
    y_specs = [pl.BlockSpec(tile, lambda q: (0, jnp.clip(q - 1, 0, n_p - 1), 0)),
               pl.BlockSpec(tile, lambda q: sample_block(q - 1))]
    state_spec = pl.BlockSpec((None,) + state_shape, lambda q: (tiling.group(q), 0, 0))
    row_tile_bytes = _nbytes((rows, D_MODEL), _F32)
    scratch = [
        ((lru_hist + rows, D_LRU), _F32),
        ((LANE_BLOCKS, ccm_hist + rows, V7X_LANES), _F32),
        ((nb, D_LRU), _F32),
        ((rows, D_MODEL), _F32),
    ]
    need = (sum(_nbytes(a.shape, a.dtype) for a in params)
            + 4 * _nbytes(state_shape, _F32)
            + 8 * row_tile_bytes
            + sum(_nbytes(s, d) for s, d in scratch)
            + 12 * row_tile_bytes + 2 * _nbytes((rows, d_ff), _F32))
    assert need <= V7X_VMEM_BYTES, need
    n_groups = tiling.n_groups
    return pl.pallas_call(
        functools.partial(_layer_kernel, tiling=tiling),
        grid=(tiling.n_tiles + 1,),
        in_specs=x_specs + [state_spec] + [_resident(a.shape) for a in params],
        out_specs=y_specs + [state_spec],
        out_shape=[jax.ShapeDtypeStruct(x_prompt.shape, _F32),
                   jax.ShapeDtypeStruct(x_sample.shape, _F32),
                   jax.ShapeDtypeStruct((n_groups,) + state_shape, _F32)],
        scratch_shapes=[pltpu.VMEM(s, d) for s, d in scratch],
        compiler_params=pltpu.CompilerParams(
            dimension_semantics=("arbitrary",), vmem_limit_bytes=int(need)),
        name="layer",
    )(x_prompt, x_sample, state, *params)


def _pack_kernel(*refs, row_blocks):
    step = pl.program_id(0)
    w_refs, o_refs = refs[:len(row_blocks)], refs[len(row_blocks):]
    for w_ref, o_ref, n_blocks in zip(w_refs, o_refs, row_blocks):
        @pl.when(step < n_blocks)
        def _pack(w_ref=w_ref, o_ref=o_ref):
            o_ref[...] = pltpu.bitcast(w_ref[...].astype(_BF16), jnp.uint32)


def _pack_rows(weights):
    row_blocks = tuple(w.shape[0] // PACK_ROWS for w in weights)
    assert all(w.shape[0] % PACK_ROWS == 0 for w in weights)

    def spec(w, rows, last):
        return pl.BlockSpec((rows, w.shape[1]), lambda s: (jnp.minimum(s, last), 0))

    block_bytes = sum(_nbytes((PACK_ROWS, w.shape[1]), _F32) for w in weights)
    return pl.pallas_call(
        functools.partial(_pack_kernel, row_blocks=row_blocks),
        grid=(max(row_blocks),),
        in_specs=[spec(w, PACK_ROWS, n - 1) for w, n in zip(weights, row_blocks)],
        out_specs=[spec(w, PACK_ROWS // 2, n - 1) for w, n in zip(weights, row_blocks)],
        out_shape=[jax.ShapeDtypeStruct((w.shape[0] // 2, w.shape[1]), jnp.uint32)
                   for w in weights],
        compiler_params=pltpu.CompilerParams(
            dimension_semantics=("arbitrary",),
            vmem_limit_bytes=4 * block_bytes),
        name="pack_rows",
    )(*weights)


def _gate_head_blocks(w):
    per = GATE_BLOCK // LRU_HEAD_DIM
    w = w.reshape(LRU_HEADS // per, per, LRU_HEAD_DIM, LRU_HEAD_DIM)
    eye = jnp.eye(per, dtype=w.dtype)
    return jnp.einsum("gpde,pq->gpdqe", w, eye).reshape(D_LRU, GATE_BLOCK)


def _group_time_major(a):
    streams, frames, c = a.shape
    a = a.reshape(streams // GROUP_STREAMS, GROUP_STREAMS, frames, c)
    return jnp.swapaxes(a, 1, 2).reshape(streams // GROUP_STREAMS, frames * GROUP_STREAMS, c)


def _group_stream_major(a):
    groups, rows, c = a.shape
    a = a.reshape(groups, rows // GROUP_STREAMS, GROUP_STREAMS, c)
    return jnp.swapaxes(a, 1, 2).reshape(groups * GROUP_STREAMS, rows // GROUP_STREAMS, c)


def kernel(x_prompt, x_sample, state_lru_h, cache_lru_conv, cache_ccm_conv, g_mix, w_in, b_in, w_lru_conv, b_lru_conv, w_rg_r, b_rg_r, w_rg_i, b_rg_i, lru_lambda, w_lru_o, w_ccm_dw, b_ccm_dw, g_ccm_ln, b_ccm_ln, w_ccm_o, w_out, g_ffn, w_ffn_gate, w_ffn_up, w_ffn_down, g_final):
    assert g_mix.shape[0] == 1, "single-layer trunk"
    row = lambda v: v.reshape(1, -1)
    (p_in, p_r, p_i, p_lo, p_co, p_out, p_gate, p_up, p_down) = _pack_rows([
        w_in[0], _gate_head_blocks(w_rg_r[0]), _gate_head_blocks(w_rg_i[0]), w_lru_o[0],
        w_ccm_o[0], w_out[0], w_ffn_gate[0], w_ffn_up[0], w_ffn_down[0]])
    params = (
        row(g_mix[0]), p_in, row(b_in[0]), w_lru_conv[0], row(b_lru_conv[0]),
        p_r, row(b_rg_r[0]), p_i, row(b_rg_i[0]),
        row(lru_lambda[0]), p_lo,
        w_ccm_dw[0], row(b_ccm_dw[0]), row(g_ccm_ln[0]), row(b_ccm_ln[0]),
        p_co, p_out,
        row(g_ffn[0]), p_gate, p_up, p_down, row(g_final))

    bp = x_prompt.shape[0]
    sample_state = jnp.concatenate(
        [state_lru_h[0][:, None, :], cache_lru_conv[0], cache_ccm_conv[0]], axis=1)
    prompt_state = jnp.zeros((bp,) + sample_state.shape[1:], x_prompt.dtype)
    state = _group_time_major(jnp.concatenate([prompt_state, sample_state], axis=0))

    y_p, y_s, state = _layer_call(x_prompt, x_sample, state, params)
    state = _group_stream_major(state)
    h_last, lru_tail, ccm_tail = state[:, 0], state[:, 1:LRU_CONV], state[:, LRU_CONV:]
    return (y_p, y_s, h_last[None, :bp], lru_tail[None, :bp], ccm_tail[None, :bp],
            h_last[None, bp:], lru_tail[None, bp:], ccm_tail[None, bp:])
```
